```python
import math
import jax, jax.numpy as jnp
from jax import lax
import numpy as np

D_MODEL = 1024
BATCH = 2
SEQ = 8192
DEPTH = 2
DEC_BATCH = 32
DEC_SEQ = 4
PAST_LEN = 16384
PAGE_SIZE = 128

N_HEADS = 16
HEAD_DIM = D_MODEL // N_HEADS
D_FF = 2816
CONV_W = 31
MOBA_BLOCK = 256
TOP_K = 3
Q_BLOCK = 128
N_BUCKETS = 32
MAX_EXACT = N_BUCKETS // 2
REL_MAX_DIST = 2048
N_A_LAYERS = DEPTH // 2
N_B_LAYERS = DEPTH - N_A_LAYERS
EPS = 1e-6
SCALE = HEAD_DIM ** -0.5

kernel_name = "yoco_conformer_conv_moba_decoder_step"


def rms_norm(x, g):
    xf = x.astype(jnp.float32)
    y = xf * lax.rsqrt(jnp.mean(xf * xf, axis=-1, keepdims=True) + EPS)
    return (y * g.astype(jnp.float32)).astype(x.dtype)


def layer_norm(x, g, b):
    xf = x.astype(jnp.float32)
    mu = jnp.mean(xf, axis=-1, keepdims=True)
    var = jnp.mean(jnp.square(xf - mu), axis=-1, keepdims=True)
    y = (xf - mu) * lax.rsqrt(var + EPS)
    return (y * g.astype(jnp.float32) + b.astype(jnp.float32)).astype(x.dtype)


def swiglu(h, w_in, w_out):
    a, g = jnp.split(h @ w_in, 2, axis=-1)
    return (jax.nn.silu(a) * g) @ w_out


def t5_bucket(delta):
    n = jnp.maximum(delta, 0)
    nf = jnp.maximum(n, 1).astype(jnp.float32)
    large = MAX_EXACT + (jnp.log(nf / MAX_EXACT) / math.log(REL_MAX_DIST / MAX_EXACT)
                         * (N_BUCKETS - MAX_EXACT)).astype(jnp.int32)
    large = jnp.minimum(large, N_BUCKETS - 1)
    return jnp.where(n < MAX_EXACT, n, large)


def conv_module(h, conv_prev, w_pw1, b_pw1, w_dw, b_dw, ln_g, ln_b, w_pw2, b_pw2):
    a, g = jnp.split(h @ w_pw1 + b_pw1, 2, axis=-1)
    v = a * jax.nn.sigmoid(g)
    vp = jnp.concatenate([conv_prev.astype(v.dtype), v], axis=1)
    y = lax.conv_general_dilated(vp, w_dw[:, None, :].astype(v.dtype), window_strides=(1,),
                                 padding='VALID', dimension_numbers=('NWC', 'WIO', 'NWC'),
                                 feature_group_count=D_MODEL) + b_dw
    y = jax.nn.silu(layer_norm(y, ln_g, ln_b))
    return y @ w_pw2 + b_pw2, vp[:, -(CONV_W - 1):]


def shared_kv(x, norm_kv, w_kv, k_gain):
    B, S, _ = x.shape
    k, v = jnp.split(rms_norm(x, norm_kv) @ w_kv, 2, axis=-1)
    k = rms_norm(k.reshape(B, S, N_HEADS, HEAD_DIM), k_gain)
    return k, v.reshape(B, S, N_HEADS, HEAD_DIM)


def moba_prompt(q, k, v, rel_bias):
    B, S, H, Dh = q.shape
    n_blk = -(-S // MOBA_BLOCK)
    pad = n_blk * MOBA_BLOCK - S
    qh = q.transpose(0, 2, 1, 3)
    kh = jnp.pad(k.transpose(0, 2, 1, 3), ((0, 0), (0, 0), (0, pad), (0, 0)))
    vh = jnp.pad(v.transpose(0, 2, 1, 3), ((0, 0), (0, 0), (0, pad), (0, 0)))
    k_blocks = kh.reshape(B, H, n_blk, MOBA_BLOCK, Dh)
    v_blocks = vh.reshape(B, H, n_blk, MOBA_BLOCK, Dh)
    k_mean = jnp.mean(k_blocks.astype(jnp.float32), axis=3)
    n_sel = min(TOP_K, n_blk - 1)
    bias_hb = rel_bias.T
    bi = jnp.arange(B)[:, None, None]
    hi = jnp.arange(H)[None, :, None]
    blk_ids = jnp.arange(n_blk)
    offs = jnp.arange(MOBA_BLOCK)

    def one_block(i):
        q0 = i * Q_BLOCK
        qb = lax.dynamic_slice_in_dim(qh, q0, Q_BLOCK, axis=2)
        qpos = q0 + jnp.arange(Q_BLOCK)
        c = q0 // MOBA_BLOCK
        k_own = lax.dynamic_index_in_dim(k_blocks, c, axis=2, keepdims=False)
        v_own = lax.dynamic_index_in_dim(v_blocks, c, axis=2, keepdims=False)
        kpos_own = c * MOBA_BLOCK + offs
        lo = jnp.einsum('bhqd,bhkd->bhqk', qb, k_own).astype(jnp.float32) * SCALE
        lo = lo + bias_hb[:, t5_bucket(qpos[:, None] - kpos_own[None, :])][None]
        lo = jnp.where(kpos_own[None, :] <= qpos[:, None], lo, -jnp.inf)
        logits = [lo]
        sels = []
        if n_sel > 0:
            gate = jnp.einsum('bhqd,bhnd->bhqn', qb.astype(jnp.float32), k_mean)
            gate = jnp.where(blk_ids < c, gate, -jnp.inf)
            _, sel = lax.top_k(gate, n_sel)
            for j in range(n_sel):
                idx = sel[..., j]
                kj = k_blocks[bi, hi, idx]
                kpos = idx[..., None] * MOBA_BLOCK + offs
                lj = jnp.einsum('bhqd,bhqkd->bhqk', qb, kj).astype(jnp.float32) * SCALE
                lj = lj + bias_hb[hi[..., None], t5_bucket(qpos[:, None] - kpos)]
                logits.append(jnp.where(j < c, lj, -jnp.inf))
                sels.append(idx)
        p = jax.nn.softmax(jnp.concatenate(logits, axis=-1), axis=-1).astype(v.dtype)
        out = jnp.einsum('bhqk,bhkd->bhqd', p[..., :MOBA_BLOCK], v_own)
        for j, idx in enumerate(sels):
            vj = v_blocks[bi, hi, idx]
            out = out + jnp.einsum('bhqk,bhqkd->bhqd',
                                   p[..., (j + 1) * MOBA_BLOCK:(j + 2) * MOBA_BLOCK], vj)
        return out

    outs = lax.map(one_block, jnp.arange(S // Q_BLOCK))
    return outs.transpose(1, 0, 3, 2, 4).reshape(B, S, H, Dh)


def moba_sample(q, k_new, v_new, cache_k, cache_v, page_table, rel_bias):
    DB, T, H, Dh = q.shape
    n_pages = page_table.shape[1]
    past = n_pages * PAGE_SIZE
    ppb = MOBA_BLOCK // PAGE_SIZE
    n_full = past // MOBA_BLOCK
    n_sel = min(TOP_K, n_full)
    bias_hb = rel_bias.T
    qh = q.transpose(0, 2, 1, 3)
    knh = k_new.transpose(0, 2, 1, 3).astype(cache_k.dtype)
    vnh = v_new.transpose(0, 2, 1, 3).astype(cache_v.dtype)
    qpos = past + jnp.arange(T)
    bi = jnp.arange(DB)[:, None, None, None]
    hi = jnp.arange(H)[None, :, None, None]
    offs = jnp.arange(MOBA_BLOCK)
    own_pt = page_table[:, n_full * ppb:]
    own_pages = own_pt.shape[1]
    k_op = cache_k[own_pt].transpose(0, 2, 1, 3, 4).reshape(DB, H, own_pages * PAGE_SIZE, Dh)
    v_op = cache_v[own_pt].transpose(0, 2, 1, 3, 4).reshape(DB, H, own_pages * PAGE_SIZE, Dh)
    k_own = jnp.concatenate([k_op, knh], axis=2)
    v_own = jnp.concatenate([v_op, vnh], axis=2)
    own_len = own_pages * PAGE_SIZE + T
    kpos_own = n_full * MOBA_BLOCK + jnp.arange(own_len)
    lo = jnp.einsum('bhqd,bhkd->bhqk', qh, k_own).astype(jnp.float32) * SCALE
    lo = lo + bias_hb[:, t5_bucket(qpos[:, None] - kpos_own[None, :])][None]
    lo = jnp.where(kpos_own[None, :] <= qpos[:, None], lo, -jnp.inf)
    logits = [lo]
    sels = []
    if n_sel > 0:
        k_full = cache_k[page_table[:, :n_full * ppb]]
        k_mean = jnp.mean(k_full.astype(jnp.float32).reshape(DB, n_full, ppb, H, PAGE_SIZE, Dh),
                          axis=(2, 4))
        gate = jnp.einsum('bhqd,bnhd->bhqn', qh.astype(jnp.float32), k_mean)
        _, sel = lax.top_k(gate, n_sel)
        for j in range(n_sel):
            idx = sel[..., j]
            phys = page_table[bi, idx[..., None] * ppb + jnp.arange(ppb)]
            kj = cache_k[phys, hi].reshape(DB, H, T, MOBA_BLOCK, Dh)
            kpos = idx[..., None] * MOBA_BLOCK + offs
            lj = jnp.einsum('bhqd,bhqkd->bhqk', qh, kj).astype(jnp.float32) * SCALE
            lj = lj + bias_hb[hi[..., 0][..., None], t5_bucket(qpos[:, None] - kpos)]
            logits.append(lj)
            sels.append(phys)
    p = jax.nn.softmax(jnp.concatenate(logits, axis=-1), axis=-1).astype(cache_v.dtype)
    out = jnp.einsum('bhqk,bhkd->bhqd', p[..., :own_len], v_own)
    for j, phys in enumerate(sels):
        vj = cache_v[phys, hi].reshape(DB, H, T, MOBA_BLOCK, Dh)
        lo_i = own_len + j * MOBA_BLOCK
        out = out + jnp.einsum('bhqk,bhqkd->bhqd', p[..., lo_i:lo_i + MOBA_BLOCK], vj)
    return out.transpose(0, 2, 1, 3).astype(q.dtype)


def trunk(x, conv_prev, attend, norm_ffn, w_ffn_in, w_ffn_out, norm_mix, w_pw1, b_pw1, w_dw,
          b_dw, ln_g, ln_b, w_pw2, b_pw2, norm_kv, w_kv, k_gain, w_q, q_gain, w_o):
    B, S, _ = x.shape
    new_conv = []
    k_sh = v_sh = None
    for l in range(DEPTH):
        x = x + 0.5 * swiglu(rms_norm(x, norm_ffn[l, 0]), w_ffn_in[l, 0], w_ffn_out[l, 0])
        h = rms_norm(x, norm_mix[l])
        if l < N_A_LAYERS:
            y, st = conv_module(h, conv_prev[l], w_pw1[l], b_pw1[l], w_dw[l], b_dw[l],
                                ln_g[l], ln_b[l], w_pw2[l], b_pw2[l])
            new_conv.append(st)
        else:
            j = l - N_A_LAYERS
            q = rms_norm((h @ w_q[j]).reshape(B, S, N_HEADS, HEAD_DIM), q_gain[j])
            y = attend(q, k_sh, v_sh).reshape(B, S, D_MODEL) @ w_o[j]
        x = x + y
        x = x + 0.5 * swiglu(rms_norm(x, norm_ffn[l, 1]), w_ffn_in[l, 1], w_ffn_out[l, 1])
        if l == N_A_LAYERS - 1:
            k_sh, v_sh = shared_kv(x, norm_kv, w_kv, k_gain)
    return x, jnp.stack(new_conv, axis=0), k_sh, v_sh


def setup_inputs(seed: int = 0) -> dict:
    key = jax.random.key(seed)
    ks = jax.random.split(key, 32)
    f32 = jnp.float32
    n_pages = PAST_LEN // PAGE_SIZE
    n_pool = (DEC_BATCH * n_pages * 5 + 3) // 4
    nrm = lambda k, s, sc: jax.random.normal(k, s, f32) * sc
    gain = lambda k, s: 1.0 + 0.05 * jax.random.normal(k, s, f32)
    perm = jax.random.permutation(ks[5], n_pool)[:DEC_BATCH * n_pages]
    return {
        "x_prompt": nrm(ks[0], (BATCH, SEQ, D_MODEL), 1.0),
        "x_sample": nrm(ks[1], (DEC_BATCH, DEC_SEQ, D_MODEL), 1.0),
        "state_conv": nrm(ks[2], (N_A_LAYERS, DEC_BATCH, CONV_W - 1, D_MODEL), 1.0),
        "cache_k": nrm(ks[3], (n_pool, N_HEADS, PAGE_SIZE, HEAD_DIM), 1.0),
        "cache_v": nrm(ks[4], (n_pool, N_HEADS, PAGE_SIZE, HEAD_DIM), 1.0),
        "page_table": perm.reshape(DEC_BATCH, n_pages).astype(jnp.int32),
        "norm_ffn": gain(ks[6], (DEPTH, 2, D_MODEL)),
        "w_ffn_in": nrm(ks[7], (DEPTH, 2, D_MODEL, 2 * D_FF), D_MODEL ** -0.5),
        "w_ffn_out": nrm(ks[8], (DEPTH, 2, D_FF, D_MODEL), D_FF ** -0.5),
        "norm_mix": gain(ks[9], (DEPTH, D_MODEL)),
        "w_pw1": nrm(ks[10], (N_A_LAYERS, D_MODEL, 2 * D_MODEL), D_MODEL ** -0.5),
        "b_pw1": nrm(ks[11], (N_A_LAYERS, 2 * D_MODEL), 0.02),
        "w_dw": nrm(ks[12], (N_A_LAYERS, CONV_W, D_MODEL), CONV_W ** -0.5),
        "b_dw": nrm(ks[13], (N_A_LAYERS, D_MODEL), 0.02),
        "ln_g": gain(ks[14], (N_A_LAYERS, D_MODEL)),
        "ln_b": nrm(ks[15], (N_A_LAYERS, D_MODEL), 0.02),
        "w_pw2": nrm(ks[16], (N_A_LAYERS, D_MODEL, D_MODEL), D_MODEL ** -0.5),
        "b_pw2": nrm(ks[17], (N_A_LAYERS, D_MODEL), 0.02),
        "norm_kv": gain(ks[18], (D_MODEL,)),
        "w_kv": nrm(ks[19], (D_MODEL, 2 * D_MODEL), D_MODEL ** -0.5),
        "k_gain": gain(ks[20], (HEAD_DIM,)),
        "w_q": nrm(ks[21], (N_B_LAYERS, D_MODEL, D_MODEL), D_MODEL ** -0.5),
        "q_gain": gain(ks[22], (N_B_LAYERS, HEAD_DIM)),
        "w_o": nrm(ks[23], (N_B_LAYERS, D_MODEL, D_MODEL), D_MODEL ** -0.5),
        "rel_bias": nrm(ks[24], (N_BUCKETS, N_HEADS), 0.5),
    }


def reference(x_prompt, x_sample, state_conv, cache_k, cache_v, page_table, norm_ffn, w_ffn_in,
              w_ffn_out, norm_mix, w_pw1, b_pw1, w_dw, b_dw, ln_g, ln_b, w_pw2, b_pw2, norm_kv,
              w_kv, k_gain, w_q, q_gain, w_o, rel_bias):
    B, S, _ = x_prompt.shape
    conv_zero = jnp.zeros((N_A_LAYERS, B, CONV_W - 1, D_MODEL), x_prompt.dtype)
    attend_prompt = lambda q, k, v: moba_prompt(q, k, v, rel_bias)
    attend_sample = lambda q, k, v: moba_sample(q, k, v, cache_k, cache_v, page_table, rel_bias)
    y_prompt, conv_prompt, k_p, v_p = trunk(
        x_prompt, conv_zero, attend_prompt, norm_ffn, w_ffn_in, w_ffn_out, norm_mix, w_pw1,
        b_pw1, w_dw, b_dw, ln_g, ln_b, w_pw2, b_pw2, norm_kv, w_kv, k_gain, w_q, q_gain, w_o)
    y_sample, conv_sample, k_s, v_s = trunk(
        x_sample, state_conv, attend_sample, norm_ffn, w_ffn_in, w_ffn_out, norm_mix, w_pw1,
        b_pw1, w_dw, b_dw, ln_g, ln_b, w_pw2, b_pw2, norm_kv, w_kv, k_gain, w_q, q_gain, w_o)
    k_prompt_pages = k_p.reshape(B, S // PAGE_SIZE, PAGE_SIZE, N_HEADS, HEAD_DIM).transpose(0, 1, 3, 2, 4)
    v_prompt_pages = v_p.reshape(B, S // PAGE_SIZE, PAGE_SIZE, N_HEADS, HEAD_DIM).transpose(0, 1, 3, 2, 4)
    k_sample_rows = k_s.transpose(0, 2, 1, 3)
    v_sample_rows = v_s.transpose(0, 2, 1, 3)
    return (y_prompt, y_sample, conv_prompt, conv_sample, k_prompt_pages, v_prompt_pages,
            k_sample_rows, v_sample_rows)
```

```python
import functools
import math

import numpy as np
import jax
import jax.numpy as jnp
from jax import lax
from jax.experimental import pallas as pl
from jax.experimental.pallas import tpu as pltpu

D_MODEL = 1024
N_HEADS = 16
HEAD_DIM = D_MODEL // N_HEADS
D_FF = 2816
CONV_W = 31
MOBA_BLOCK = 256
TOP_K = 3
PAGE_SIZE = 128
PAGES_PER_BLOCK = MOBA_BLOCK // PAGE_SIZE
N_BUCKETS = 32
MAX_EXACT = N_BUCKETS // 2
REL_MAX_DIST = 2048
EPS = 1e-6
SCALE = HEAD_DIM ** -0.5

LANES = 128
SUBLANES = 8
VMEM_LIMIT = 56 * 1024 * 1024

F32 = jnp.float32
BF16 = jnp.bfloat16
NEG_INF = float("-inf")

CONV_HALO = 32
CONV_PAD = CONV_HALO - (CONV_W - 1)


def _first_far_distance():
    n = np.arange(1, 4 * REL_MAX_DIST, dtype=np.float32)
    large = MAX_EXACT + (np.log(n / np.float32(MAX_EXACT)) / np.float32(math.log(REL_MAX_DIST / MAX_EXACT))
                         * np.float32(N_BUCKETS - MAX_EXACT)).astype(np.int32)
    bucket = np.where(n < MAX_EXACT, n.astype(np.int32), np.minimum(large, N_BUCKETS - 1))
    return int(n[np.argmax(bucket == N_BUCKETS - 1)])


FAR_DIST = _first_far_distance()
FAR_BLOCKS = -(-(FAR_DIST + MOBA_BLOCK - 1) // MOBA_BLOCK)


def _cparams(*sem):
    return pltpu.CompilerParams(dimension_semantics=sem, vmem_limit_bytes=VMEM_LIMIT)


def _rms(x, g):
    ms = jnp.mean(x * x, axis=-1, keepdims=True)
    return x * lax.rsqrt(ms + EPS) * g


def _t5_bucket(delta):
    n = jnp.maximum(delta, 0)
    nf = jnp.maximum(n, 1).astype(F32)
    large = MAX_EXACT + (jnp.log(nf / MAX_EXACT) / math.log(REL_MAX_DIST / MAX_EXACT)
                         * (N_BUCKETS - MAX_EXACT)).astype(jnp.int32)
    large = jnp.minimum(large, N_BUCKETS - 1)
    return jnp.where(n < MAX_EXACT, n, large)


def _bias_from_bucket(bucket, tab_ref, h):
    val = jnp.zeros(bucket.shape, F32)
    for bb in range(N_BUCKETS):
        val = jnp.where(bucket == bb, tab_ref[h, bb], val)
    return val


def _ffn_kernel(x_ref, g_ref, win_ref, wout_ref, o_ref, h_ref, acc_ref, *, tf):
    x = x_ref[...]
    h_ref[...] = _rms(x, g_ref[...]).astype(BF16)
    acc_ref[...] = jnp.zeros_like(acc_ref)

    def body(j, carry):
        off = pl.multiple_of(j * tf, tf)
        off_g = pl.multiple_of(D_FF + j * tf, LANES)
        h = h_ref[...]
        a = jnp.dot(h, win_ref[:, pl.ds(off, tf)], preferred_element_type=F32)
        g = jnp.dot(h, win_ref[:, pl.ds(off_g, tf)], preferred_element_type=F32)
        act = (a * jax.nn.sigmoid(a) * g).astype(BF16)
        acc_ref[...] += jnp.dot(act, wout_ref[pl.ds(off, tf), :], preferred_element_type=F32)
        return carry

    lax.fori_loop(0, D_FF // tf, body, 0)
    o_ref[...] = x + 0.5 * acc_ref[...]


def ffn(x, g, w_in, w_out, *, tm, tf=256):
    t = x.shape[0]
    return pl.pallas_call(
        functools.partial(_ffn_kernel, tf=tf),
        grid=(t // tm,),
        in_specs=[
            pl.BlockSpec((tm, D_MODEL), lambda i: (i, 0)),
            pl.BlockSpec((1, D_MODEL), lambda i: (0, 0)),
            pl.BlockSpec((D_MODEL, 2 * D_FF), lambda i: (0, 0), pipeline_mode=pl.Buffered(1)),
            pl.BlockSpec((D_FF, D_MODEL), lambda i: (0, 0), pipeline_mode=pl.Buffered(1)),
        ],
        out_specs=pl.BlockSpec((tm, D_MODEL), lambda i: (i, 0)),
        out_shape=jax.ShapeDtypeStruct((t, D_MODEL), F32),
        scratch_shapes=[pltpu.VMEM((tm, D_MODEL), BF16), pltpu.VMEM((tm, D_MODEL), F32)],
        compiler_params=_cparams("parallel"),
        name="ffn",
    )(x, g.reshape(1, D_MODEL), w_in, w_out)


def _glu_kernel(x_ref, g_ref, w_ref, b_ref, o_ref):
    h = _rms(x_ref[...], g_ref[...]).astype(BF16)
    ag = jnp.dot(h, w_ref[...], preferred_element_type=F32) + b_ref[...]
    a = ag[:, :D_MODEL]
    gate = ag[:, D_MODEL:]
    o_ref[...] = a * jax.nn.sigmoid(gate)


def glu(x, g, w, b, *, tm):
    t = x.shape[0]
    return pl.pallas_call(
        _glu_kernel,
        grid=(t // tm,),
        in_specs=[
            pl.BlockSpec((tm, D_MODEL), lambda i: (i, 0)),
            pl.BlockSpec((1, D_MODEL), lambda i: (0, 0)),
            pl.BlockSpec((D_MODEL, 2 * D_MODEL), lambda i: (0, 0)),
            pl.BlockSpec((1, 2 * D_MODEL), lambda i: (0, 0)),
        ],
        out_specs=pl.BlockSpec((tm, D_MODEL), lambda i: (i, 0)),
        out_shape=jax.ShapeDtypeStruct((t, D_MODEL), F32),
        compiler_params=_cparams("parallel"),
        name="glu",
    )(x, g.reshape(1, D_MODEL), w, b.reshape(1, 2 * D_MODEL))


def _dwconv_kernel(prev_ref, v_ref, wdw_ref, bdw_ref, lng_ref, lnb_ref, y_ref, state_ref, buf_ref,
                   *, ts, valid):
    @pl.when(pl.program_id(1) == 0)
    def _():
        buf_ref[0:CONV_HALO, :] = prev_ref[...]

    buf_ref[CONV_HALO:CONV_HALO + ts, :] = v_ref[...]
    for c in range(D_MODEL // LANES):
        lanes = slice(c * LANES, (c + 1) * LANES)
        acc = jnp.zeros((ts, LANES), F32)
        for r in range(SUBLANES):
            taps = range(r, CONV_W, SUBLANES)
            xr = buf_ref[CONV_PAD + r:CONV_PAD + taps[-1] + ts, lanes]
            for w in taps:
                acc = acc + xr[w - r:w - r + ts, :] * wdw_ref[w:w + 1, lanes]
        y_ref[:, lanes] = acc + bdw_ref[:, lanes]
    y = y_ref[...]
    mu = jnp.mean(y, axis=-1, keepdims=True)
    var = jnp.mean(jnp.square(y - mu), axis=-1, keepdims=True)
    yn = (y - mu) * lax.rsqrt(var + EPS) * lng_ref[...] + lnb_ref[...]
    y_ref[...] = yn * jax.nn.sigmoid(yn)
    state_ref[...] = buf_ref[valid:valid + CONV_HALO, :]
    buf_ref[0:CONV_HALO, :] = buf_ref[ts:ts + CONV_HALO, :]


def dwconv(prev, v, w_dw, b_dw, ln_g, ln_b, *, ts, valid):
    b, s, _ = v.shape
    row = lambda a: a.reshape(1, D_MODEL)
    return pl.pallas_call(
        functools.partial(_dwconv_kernel, ts=ts, valid=valid),
        grid=(b, s // ts),
        in_specs=[
            pl.BlockSpec((None, CONV_HALO, D_MODEL), lambda i, j: (i, 0, 0)),
            pl.BlockSpec((None, ts, D_MODEL), lambda i, j: (i, j, 0)),
            pl.BlockSpec((CONV_W, D_MODEL), lambda i, j: (0, 0)),
            pl.BlockSpec((1, D_MODEL), lambda i, j: (0, 0)),
            pl.BlockSpec((1, D_MODEL), lambda i, j: (0, 0)),
            pl.BlockSpec((1, D_MODEL), lambda i, j: (0, 0)),
        ],
        out_specs=[
            pl.BlockSpec((None, ts, D_MODEL), lambda i, j: (i, j, 0)),
            pl.BlockSpec((None, CONV_HALO, D_MODEL), lambda i, j: (i, 0, 0)),
        ],
        out_shape=[
            jax.ShapeDtypeStruct((b, s, D_MODEL), F32),
            jax.ShapeDtypeStruct((b, CONV_HALO, D_MODEL), F32),
        ],
        scratch_shapes=[pltpu.VMEM((CONV_HALO + ts, D_MODEL), F32)],
        compiler_params=_cparams("parallel", "arbitrary"),
        name="dwconv",
    )(prev, v, w_dw, row(b_dw), row(ln_g), row(ln_b))


def _linear_res_kernel(y_ref, w_ref, b_ref, x_ref, o_ref, *, transposed):
    y = y_ref[...]
    if transposed:
        y = y.T
    o_ref[...] = x_ref[...] + jnp.dot(y.astype(BF16), w_ref[...], preferred_element_type=F32) + b_ref[...]


def linear_res(y, w, b, x, *, tm, transposed=False):
    t = x.shape[0]
    if transposed:
        tiles = y.shape[2] // tm
        y_spec = pl.BlockSpec((None, D_MODEL, tm), lambda i: (i // tiles, 0, i % tiles))
    else:
        y_spec = pl.BlockSpec((tm, D_MODEL), lambda i: (i, 0))
    return pl.pallas_call(
        functools.partial(_linear_res_kernel, transposed=transposed),
        grid=(t // tm,),
        in_specs=[
            y_spec,
            pl.BlockSpec((D_MODEL, D_MODEL), lambda i: (0, 0)),
            pl.BlockSpec((1, D_MODEL), lambda i: (0, 0)),
            pl.BlockSpec((tm, D_MODEL), lambda i: (i, 0)),
        ],
        out_specs=pl.BlockSpec((tm, D_MODEL), lambda i: (i, 0)),
        out_shape=jax.ShapeDtypeStruct((t, D_MODEL), F32),
        compiler_params=_cparams("parallel"),
        name="linear_res",
    )(y, w, b.reshape(1, D_MODEL), x)


def _kv_kernel(x_ref, g_ref, w_ref, kgain_ref, k_ref, v_ref, vt_ref, kmean_ref):
    h = _rms(x_ref[...], g_ref[...]).astype(BF16)
    kv = jnp.dot(h, w_ref[...], preferred_element_type=F32)
    k = kv[:, :D_MODEL]
    v = kv[:, D_MODEL:]
    vt_ref[...] = v.T
    for hd in range(N_HEADS):
        cols = slice(hd * HEAD_DIM, (hd + 1) * HEAD_DIM)
        kh = _rms(k[:, cols], kgain_ref[...])
        vh = v[:, cols]
        kmean_ref[hd:hd + 1, :] = jnp.mean(kh, axis=0, keepdims=True)
        for p in range(PAGES_PER_BLOCK):
            rows = slice(p * PAGE_SIZE, (p + 1) * PAGE_SIZE)
            k_ref[p, hd, :, :] = kh[rows, :]
            v_ref[p, hd, :, :] = vh[rows, :]


def shared_kv(x, g, w, k_gain):
    b, s, _ = x.shape
    n_blk = s // MOBA_BLOCK
    pages = (b, s // PAGE_SIZE, N_HEADS, PAGE_SIZE, HEAD_DIM)
    page_spec = pl.BlockSpec((None, PAGES_PER_BLOCK, N_HEADS, PAGE_SIZE, HEAD_DIM),
                             lambda i, j: (i, j, 0, 0, 0))
    return pl.pallas_call(
        _kv_kernel,
        grid=(b, n_blk),
        in_specs=[
            pl.BlockSpec((None, MOBA_BLOCK, D_MODEL), lambda i, j: (i, j, 0)),
            pl.BlockSpec((1, D_MODEL), lambda i, j: (0, 0)),
            pl.BlockSpec((D_MODEL, 2 * D_MODEL), lambda i, j: (0, 0)),
            pl.BlockSpec((1, HEAD_DIM), lambda i, j: (0, 0)),
        ],
        out_specs=[
            page_spec,
            page_spec,
            pl.BlockSpec((None, D_MODEL, MOBA_BLOCK), lambda i, j: (i, 0, j)),
            pl.BlockSpec((None, None, N_HEADS, HEAD_DIM), lambda i, j: (i, j, 0, 0)),
        ],
        out_shape=[
            jax.ShapeDtypeStruct(pages, F32),
            jax.ShapeDtypeStruct(pages, F32),
            jax.ShapeDtypeStruct((b, D_MODEL, s), F32),
            jax.ShapeDtypeStruct((b, n_blk, N_HEADS, HEAD_DIM), F32),
        ],
        compiler_params=_cparams("parallel", "parallel"),
        name="shared_kv",
    )(x, g.reshape(1, D_MODEL), w, k_gain.reshape(1, HEAD_DIM))


def _q_kernel(x_ref, g_ref, w_ref, qgain_ref, qt_ref):
    h = _rms(x_ref[...], g_ref[...]).astype(BF16)
    qt = jnp.dot(h, w_ref[...], preferred_element_type=F32).T
    for hd in range(N_HEADS):
        rows = slice(hd * HEAD_DIM, (hd + 1) * HEAD_DIM)
        qh = qt[rows, :]
        ms = jnp.mean(qh * qh, axis=0, keepdims=True)
        qt_ref[rows, :] = qh * lax.rsqrt(ms + EPS) * (qgain_ref[...] * SCALE)


def q_proj(x, g, w, q_gain, *, tm):
    b, s, _ = x.shape
    return pl.pallas_call(
        _q_kernel,
        grid=(b, s // tm),
        in_specs=[
            pl.BlockSpec((None, tm, D_MODEL), lambda i, j: (i, j, 0)),
            pl.BlockSpec((1, D_MODEL), lambda i, j: (0, 0)),
            pl.BlockSpec((D_MODEL, D_MODEL), lambda i, j: (0, 0)),
            pl.BlockSpec((HEAD_DIM, 1), lambda i, j: (0, 0)),
        ],
        out_specs=pl.BlockSpec((None, D_MODEL, tm), lambda i, j: (i, 0, j)),
        out_shape=jax.ShapeDtypeStruct((b, D_MODEL, s), F32),
        compiler_params=_cparams("parallel", "parallel"),
        name="q_proj",
    )(x, g.reshape(1, D_MODEL), w, q_gain.reshape(HEAD_DIM, 1))


def _moba_prompt_kernel(tab_ref, qt_ref, k_ref, vt_ref, kmean_ref, o_ref, bias_ref, sel_ref, *, n_blk):
    h = pl.program_id(0)
    blk = MOBA_BLOCK
    kk = lax.broadcasted_iota(jnp.int32, (blk, blk), 0)
    qq = lax.broadcasted_iota(jnp.int32, (blk, blk), 1)

    @pl.when(pl.program_id(1) == 0)
    def _():
        for d in range(FAR_BLOCKS):
            bias_ref[d] = _bias_from_bucket(_t5_bucket(d * blk + qq - kk), tab_ref, h)
        bias_ref[FAR_BLOCKS] = jnp.full((blk, blk), tab_ref[h, N_BUCKETS - 1], F32)

    blk_id = lax.broadcasted_iota(jnp.int32, (n_blk, blk), 0)
    blk_f = blk_id.astype(F32)

    def q_block(c, carry):
        q0 = pl.multiple_of(c * blk, blk)
        qt = qt_ref[:, pl.ds(q0, blk)]
        qtb = qt.astype(BF16)
        gate = jnp.dot(kmean_ref[...], qt, preferred_element_type=F32, precision=lax.Precision.HIGHEST)
        past = blk_id < c
        g = jnp.where(past, gate, NEG_INF)
        addmask = jnp.full((n_blk, blk), NEG_INF, F32)
        for _ in range(TOP_K):
            top = jnp.max(g, axis=0, keepdims=True)
            first = jnp.min(jnp.where(g == top, blk_f, float(n_blk)), axis=0, keepdims=True)
            pick = blk_f == first
            addmask = jnp.where(pick, jnp.where(past, 0.0, NEG_INF), addmask)
            g = jnp.where(pick, NEG_INF, g)
        sel_ref[...] = addmask

        def scores(kb):
            kblk = k_ref[pl.ds(kb * PAGES_PER_BLOCK, PAGES_PER_BLOCK)].reshape(blk, HEAD_DIM)
            return jnp.dot(kblk.astype(BF16), qtb, preferred_element_type=F32)

        def values(kb):
            return vt_ref[:, pl.ds(pl.multiple_of(kb * blk, blk), blk)].astype(BF16)

        s = jnp.where(kk <= qq, scores(c) + bias_ref[0], NEG_INF)
        m = jnp.max(s, axis=0, keepdims=True)
        p = jnp.exp(s - m)
        l = jnp.sum(p, axis=0, keepdims=True)
        acc = jnp.dot(values(c), p.astype(BF16), preferred_element_type=F32)

        def kv_step(kb, mla):
            m, l, acc = mla
            d = jnp.minimum(c - kb, FAR_BLOCKS)
            s = scores(kb) + bias_ref[d] + sel_ref[pl.ds(kb, 1), :]
            m_new = jnp.maximum(m, jnp.max(s, axis=0, keepdims=True))
            alpha = jnp.exp(m - m_new)
            p = jnp.exp(s - m_new)
            l = alpha * l + jnp.sum(p, axis=0, keepdims=True)
            acc = alpha * acc + jnp.dot(values(kb), p.astype(BF16), preferred_element_type=F32)
            return m_new, l, acc

        m, l, acc = lax.fori_loop(0, c, kv_step, (m, l, acc))
        o_ref[:, pl.ds(q0, blk)] = acc / l
        return carry

    lax.fori_loop(0, n_blk, q_block, 0)


def moba_prompt(tab, qt, k_pages, vt, kmean):
    b, _, s = qt.shape
    n_blk = s // MOBA_BLOCK
    head_t = pl.BlockSpec((None, HEAD_DIM, s), lambda h, i: (i, h, 0))
    return pl.pallas_call(
        functools.partial(_moba_prompt_kernel, n_blk=n_blk),
        grid=(N_HEADS, b),
        in_specs=[
            pl.BlockSpec(memory_space=pltpu.SMEM),
            head_t,
            pl.BlockSpec((None, s // PAGE_SIZE, None, PAGE_SIZE, HEAD_DIM), lambda h, i: (i, 0, h, 0, 0)),
            head_t,
            pl.BlockSpec((None, None, n_blk, HEAD_DIM), lambda h, i: (i, h, 0, 0)),
        ],
        out_specs=head_t,
        out_shape=jax.ShapeDtypeStruct((b, D_MODEL, s), F32),
        scratch_shapes=[
            pltpu.VMEM((FAR_BLOCKS + 1, MOBA_BLOCK, MOBA_BLOCK), F32),
            pltpu.VMEM((n_blk, MOBA_BLOCK), F32),
        ],
        compiler_params=_cparams("arbitrary", "arbitrary"),
        name="moba_prompt",
    )(tab, qt, k_pages, vt, kmean)


PAGES_PER_STEP = 8


def _page_mean_kernel(pt_ref, *refs):
    pages, o_ref = refs[:-1], refs[-1]
    for j in range(PAGES_PER_STEP // PAGES_PER_BLOCK):
        tot = jnp.zeros((N_HEADS, HEAD_DIM), F32)
        for p in range(PAGES_PER_BLOCK):
            tot = tot + jnp.sum(pages[j * PAGES_PER_BLOCK + p][...], axis=1)
        o_ref[j] = tot * (1.0 / MOBA_BLOCK)


def page_block_means(cache_k, page_table):
    db, n_pages = page_table.shape
    assert n_pages % PAGES_PER_STEP == 0
    steps = n_pages // PAGES_PER_STEP
    blocks_per_step = PAGES_PER_STEP // PAGES_PER_BLOCK

    def page_spec(p):
        return pl.BlockSpec((None, N_HEADS, PAGE_SIZE, HEAD_DIM),
                            lambda i, j, pt: (pt[i * n_pages + j * PAGES_PER_STEP + p], 0, 0, 0))

    return pl.pallas_call(
        _page_mean_kernel,
        grid_spec=pltpu.PrefetchScalarGridSpec(
            num_scalar_prefetch=1,
            grid=(db, steps),
            in_specs=[page_spec(p) for p in range(PAGES_PER_STEP)],
            out_specs=pl.BlockSpec((None, blocks_per_step, N_HEADS, HEAD_DIM), lambda i, j, pt: (i, j, 0, 0)),
        ),
        out_shape=jax.ShapeDtypeStruct((db, n_pages // PAGES_PER_BLOCK, N_HEADS, HEAD_DIM), F32),
        compiler_params=_cparams("parallel", "parallel"),
        name="page_block_means",
    )(page_table.reshape(-1), *([cache_k] * PAGES_PER_STEP))


def _gate_topk_kernel(q_ref, kmean_ref, seg_ref, o_ref, *, n_full, t_new):
    kmean = kmean_ref[...]
    blk_f = lax.broadcasted_iota(jnp.int32, (n_full, N_HEADS), 0).astype(F32)
    for t in range(t_new):
        prod = kmean * q_ref[t:t + 1, :]
        g = jnp.dot(prod, seg_ref[...], preferred_element_type=F32, precision=lax.Precision.HIGHEST)
        for j in range(TOP_K):
            top = jnp.max(g, axis=0, keepdims=True)
            first = jnp.min(jnp.where(g == top, blk_f, float(n_full)), axis=0, keepdims=True)
            o_ref[t, j:j + 1, :] = first.astype(jnp.int32)
            g = jnp.where(blk_f == first, NEG_INF, g)


def gate_topk(q, kmean):
    db, t_new, _ = q.shape
    n_full = kmean.shape[1]
    seg = (jnp.arange(D_MODEL)[:, None] // HEAD_DIM == jnp.arange(N_HEADS)[None, :]).astype(F32)
    return pl.pallas_call(
        functools.partial(_gate_topk_kernel, n_full=n_full, t_new=t_new),
        grid=(db,),
        in_specs=[
            pl.BlockSpec((None, t_new, D_MODEL), lambda i: (i, 0, 0)),
            pl.BlockSpec((None, n_full, D_MODEL), lambda i: (i, 0, 0)),
            pl.BlockSpec((D_MODEL, N_HEADS), lambda i: (0, 0)),
        ],
        out_specs=pl.BlockSpec((None, t_new, TOP_K, N_HEADS), lambda i: (i, 0, 0, 0)),
        out_shape=jax.ShapeDtypeStruct((db, t_new, TOP_K, N_HEADS), jnp.int32),
        compiler_params=_cparams("parallel"),
        name="gate_topk",
    )(q, kmean, seg)


def _moba_sample_kernel(phys_ref, sel_ref, tab_ref, q_ref, kn_ref, vn_ref, *refs, t_new, past):
    n_sel_pages = t_new * TOP_K * PAGES_PER_BLOCK
    k_pages, v_pages, o_ref = refs[:n_sel_pages], refs[n_sel_pages:2 * n_sel_pages], refs[-1]
    b = pl.program_id(0)
    h = pl.program_id(1)
    new_id = lax.broadcasted_iota(jnp.int32, (t_new, 1), 0)
    row_id = lax.broadcasted_iota(jnp.int32, (PAGE_SIZE, 1), 0)
    k_new = kn_ref[...]
    v_new = vn_ref[...]
    for t in range(t_new):
        q = q_ref[t:t + 1, :]
        s_own = jnp.sum(k_new * q, axis=-1, keepdims=True)
        s_own = s_own + _bias_from_bucket(_t5_bucket(t - new_id), tab_ref, h)
        pieces = [(jnp.where(new_id <= t, s_own, NEG_INF), v_new)]
        for j in range(TOP_K):
            idx = sel_ref[((b * N_HEADS + h) * t_new + t) * TOP_K + j]
            for p in range(PAGES_PER_BLOCK):
                slot = (t * TOP_K + j) * PAGES_PER_BLOCK + p
                s = jnp.sum(k_pages[slot][...] * q, axis=-1, keepdims=True)
                d0 = past + t - (idx * MOBA_BLOCK + p * PAGE_SIZE)
                bias = lax.cond(
                    d0 - (PAGE_SIZE - 1) >= FAR_DIST,
                    lambda: jnp.full((PAGE_SIZE, 1), tab_ref[h, N_BUCKETS - 1], F32),
                    lambda d0=d0: _bias_from_bucket(_t5_bucket(d0 - row_id), tab_ref, h))
                pieces.append((s + bias, v_pages[slot][...]))
        m = functools.reduce(jnp.maximum, [jnp.max(s, axis=0, keepdims=True) for s, _ in pieces])
        num = jnp.zeros((1, HEAD_DIM), F32)
        den = jnp.zeros((1, 1), F32)
        for s, v in pieces:
            e = jnp.exp(s - m)
            den = den + jnp.sum(e, axis=0, keepdims=True)
            num = num + jnp.sum(e * v, axis=0, keepdims=True)
        o_ref[t:t + 1, :] = num / den


def moba_sample(q, k_new, v_new, cache_k, cache_v, phys, sel, tab, *, past):
    db, _, t_new, _ = q.shape
    n_sel_pages = t_new * TOP_K * PAGES_PER_BLOCK
    row_spec = pl.BlockSpec((None, None, t_new, HEAD_DIM), lambda i, h, ph, se: (i, h, 0, 0))

    def page_spec(slot):
        return pl.BlockSpec((None, None, PAGE_SIZE, HEAD_DIM),
                            lambda i, h, ph, se: (ph[(i * N_HEADS + h) * n_sel_pages + slot], h, 0, 0))

    pages = [page_spec(s) for s in range(n_sel_pages)]
    return pl.pallas_call(
        functools.partial(_moba_sample_kernel, t_new=t_new, past=past),
        grid_spec=pltpu.PrefetchScalarGridSpec(
            num_scalar_prefetch=2,
            grid=(db, N_HEADS),
            in_specs=[pl.BlockSpec(memory_space=pltpu.SMEM), row_spec, row_spec, row_spec] + pages + pages,
            out_specs=row_spec,
        ),
        out_shape=jax.ShapeDtypeStruct((db, N_HEADS, t_new, HEAD_DIM), F32),
        compiler_params=_cparams("parallel", "parallel"),
        name="moba_sample",
    )(phys, sel, tab, q, k_new, v_new, *([cache_k] * n_sel_pages), *([cache_v] * n_sel_pages))


def _bf16_weights(w_ffn_in, w_ffn_out, w_pw1, w_pw2, w_kv, w_q, w_o):
    return tuple(w.astype(BF16) for w in (w_ffn_in, w_ffn_out, w_pw1, w_pw2, w_kv, w_q, w_o))


def kernel(x_prompt, x_sample, state_conv, cache_k, cache_v, page_table, norm_ffn, w_ffn_in, w_ffn_out,
           norm_mix, w_pw1, b_pw1, w_dw, b_dw, ln_g, ln_b, w_pw2, b_pw2, norm_kv, w_kv, k_gain, w_q, q_gain,
           w_o, rel_bias):
    bsz, seq, _ = x_prompt.shape
    dbs, t_new, _ = x_sample.shape
    n_pages = page_table.shape[1]
    past = n_pages * PAGE_SIZE
    assert past % MOBA_BLOCK == 0 and t_new <= SUBLANES and seq % MOBA_BLOCK == 0
    n_tok_s = dbs * t_new
    w_ffn_in, w_ffn_out, w_pw1, w_pw2, w_kv, w_q, w_o = _bf16_weights(
        w_ffn_in, w_ffn_out, w_pw1, w_pw2, w_kv, w_q, w_o)
    tab = rel_bias.T
    zero_bias = jnp.zeros((D_MODEL,), F32)

    xp = x_prompt.reshape(bsz * seq, D_MODEL)
    xs = x_sample.reshape(n_tok_s, D_MODEL)
    tm_p, tm_s = 512, n_tok_s

    xp = ffn(xp, norm_ffn[0, 0], w_ffn_in[0, 0], w_ffn_out[0, 0], tm=tm_p)
    xs = ffn(xs, norm_ffn[0, 0], w_ffn_in[0, 0], w_ffn_out[0, 0], tm=tm_s)

    vp = glu(xp, norm_mix[0], w_pw1[0], b_pw1[0], tm=tm_p).reshape(bsz, seq, D_MODEL)
    vs = glu(xs, norm_mix[0], w_pw1[0], b_pw1[0], tm=tm_s).reshape(dbs, t_new, D_MODEL)
    prev_p = jnp.zeros((bsz, CONV_HALO, D_MODEL), F32)
    prev_s = jnp.pad(state_conv[0], ((0, 0), (CONV_PAD, 0), (0, 0)))
    vs_pad = jnp.pad(vs, ((0, 0), (0, SUBLANES - t_new), (0, 0)))
    yp, st_p = dwconv(prev_p, vp, w_dw[0], b_dw[0], ln_g[0], ln_b[0], ts=PAGE_SIZE, valid=PAGE_SIZE)
    ys, st_s = dwconv(prev_s, vs_pad, w_dw[0], b_dw[0], ln_g[0], ln_b[0], ts=SUBLANES, valid=t_new)
    conv_prompt = st_p[None, :, CONV_PAD:]
    conv_sample = st_s[None, :, CONV_PAD:]
    xp = linear_res(yp.reshape(bsz * seq, D_MODEL), w_pw2[0], b_pw2[0], xp, tm=tm_p)
    xs = linear_res(ys[:, :t_new].reshape(n_tok_s, D_MODEL), w_pw2[0], b_pw2[0], xs, tm=tm_s)

    xp = ffn(xp, norm_ffn[0, 1], w_ffn_in[0, 1], w_ffn_out[0, 1], tm=tm_p)
    xs = ffn(xs, norm_ffn[0, 1], w_ffn_in[0, 1], w_ffn_out[0, 1], tm=tm_s)

    k_pp, v_pp, vt_p, kmean_p = shared_kv(xp.reshape(bsz, seq, D_MODEL), norm_kv, w_kv, k_gain)
    xs_blk = jnp.pad(xs, ((0, MOBA_BLOCK - n_tok_s), (0, 0)))[None]
    k_sp, v_sp, _, _ = shared_kv(xs_blk, norm_kv, w_kv, k_gain)

    def sample_rows(pages):
        rows = pages[0, 0, :, :n_tok_s].reshape(N_HEADS, dbs, t_new, HEAD_DIM)
        return rows.transpose(1, 0, 2, 3)

    k_rows, v_rows = sample_rows(k_sp), sample_rows(v_sp)

    xp = ffn(xp, norm_ffn[1, 0], w_ffn_in[1, 0], w_ffn_out[1, 0], tm=tm_p)
    xs = ffn(xs, norm_ffn[1, 0], w_ffn_in[1, 0], w_ffn_out[1, 0], tm=tm_s)

    qt_p = q_proj(xp.reshape(bsz, seq, D_MODEL), norm_mix[1], w_q[0], q_gain[0], tm=tm_p)
    at_p = moba_prompt(tab, qt_p, k_pp, vt_p, kmean_p.transpose(0, 2, 1, 3))
    xp = linear_res(at_p, w_o[0], zero_bias, xp, tm=tm_p, transposed=True)

    qt_s = q_proj(xs[None], norm_mix[1], w_q[0], q_gain[0], tm=n_tok_s)[0]
    q_s = qt_s.T.reshape(dbs, t_new, D_MODEL)
    kmean_s = page_block_means(cache_k, page_table[:, :past // PAGE_SIZE])
    sel = gate_topk(q_s, kmean_s.reshape(dbs, -1, D_MODEL))
    sel = sel.transpose(0, 3, 1, 2)
    phys = page_table[jnp.arange(dbs)[:, None, None, None, None],
                      sel[..., None] * PAGES_PER_BLOCK + jnp.arange(PAGES_PER_BLOCK)]
    q_rows = q_s.reshape(dbs, t_new, N_HEADS, HEAD_DIM).transpose(0, 2, 1, 3)
    a_s = moba_sample(q_rows, k_rows, v_rows, cache_k, cache_v, phys.reshape(-1), sel.reshape(-1), tab,
                      past=past)
    a_s = a_s.transpose(0, 2, 1, 3).reshape(n_tok_s, D_MODEL)
    xs = linear_res(a_s, w_o[0], zero_bias, xs, tm=tm_s)

    xp = ffn(xp, norm_ffn[1, 1], w_ffn_in[1, 1], w_ffn_out[1, 1], tm=tm_p)
    xs = ffn(xs, norm_ffn[1, 1], w_ffn_in[1, 1], w_ffn_out[1, 1], tm=tm_s)

    return (xp.reshape(bsz, seq, D_MODEL), xs.reshape(dbs, t_new, D_MODEL), conv_prompt, conv_sample,
            k_pp, v_pp, k_rows, v_rows)
```

```python
import functools
import math

import numpy as np
import jax
import jax.numpy as jnp
from jax import lax
from jax.experimental import pallas as pl
from jax.experimental.pallas import tpu as pltpu

D_MODEL = 1024
N_HEADS = 16
HEAD_DIM = D_MODEL // N_HEADS
D_FF = 2816
CONV_W = 31
MOBA_BLOCK = 256
TOP_K = 3
PAGE_SIZE = 128
PAGES_PER_BLOCK = MOBA_BLOCK // PAGE_SIZE
N_BUCKETS = 32
MAX_EXACT = N_BUCKETS // 2
REL_MAX_DIST = 2048
EPS = 1e-6
SCALE = HEAD_DIM ** -0.5

LANES = 128
SUBLANES = 8
VMEM_LIMIT = 56 * 1024 * 1024

F32 = jnp.float32
BF16 = jnp.bfloat16
NEG_INF = float("-inf")

CONV_HALO = 32
CONV_PAD = CONV_HALO - (CONV_W - 1)


def _first_far_distance():
    n = np.arange(1, 4 * REL_MAX_DIST, dtype=np.float32)
    large = MAX_EXACT + (np.log(n / np.float32(MAX_EXACT)) / np.float32(math.log(REL_MAX_DIST / MAX_EXACT))
                         * np.float32(N_BUCKETS - MAX_EXACT)).astype(np.int32)
    bucket = np.where(n < MAX_EXACT, n.astype(np.int32), np.minimum(large, N_BUCKETS - 1))
    return int(n[np.argmax(bucket == N_BUCKETS - 1)])


FAR_DIST = _first_far_distance()
FAR_BLOCKS = -(-(FAR_DIST + MOBA_BLOCK - 1) // MOBA_BLOCK)


def _cparams(*sem):
    return pltpu.CompilerParams(dimension_semantics=sem, vmem_limit_bytes=VMEM_LIMIT)


def _rms(x, g):
    ms = jnp.mean(x * x, axis=-1, keepdims=True)
    return x * lax.rsqrt(ms + EPS) * g


def _t5_bucket(delta):
    n = jnp.maximum(delta, 0)
    nf = jnp.maximum(n, 1).astype(F32)
    large = MAX_EXACT + (jnp.log(nf / MAX_EXACT) / math.log(REL_MAX_DIST / MAX_EXACT)
                         * (N_BUCKETS - MAX_EXACT)).astype(jnp.int32)
    large = jnp.minimum(large, N_BUCKETS - 1)
    return jnp.where(n < MAX_EXACT, n, large)


def _bias_from_bucket(bucket, tab_ref, h):
    val = jnp.zeros(bucket.shape, F32)
    for bb in range(N_BUCKETS):
        val = jnp.where(bucket == bb, tab_ref[h, bb], val)
    return val


def _ffn_kernel(x_ref, g_ref, win_ref, wout_ref, o_ref, h_ref, *, tf):
    x = x_ref[...]
    h_ref[...] = _rms(x, g_ref[...]).astype(BF16)
    acc = None
    for j in range(D_FF // tf):
        h = h_ref[...]
        a = jnp.dot(h, win_ref[:, j * tf:(j + 1) * tf], preferred_element_type=F32)
        g = jnp.dot(h, win_ref[:, D_FF + j * tf:D_FF + (j + 1) * tf], preferred_element_type=F32)
        act = (a * jax.nn.sigmoid(a) * g).astype(BF16)
        part = jnp.dot(act, wout_ref[j * tf:(j + 1) * tf, :], preferred_element_type=F32)
        acc = part if acc is None else acc + part
    o_ref[...] = x + 0.5 * acc


def ffn(x, g, w_in, w_out, *, tm, tf=D_FF // 2):
    t = x.shape[0]
    return pl.pallas_call(
        functools.partial(_ffn_kernel, tf=tf),
        grid=(t // tm,),
        in_specs=[
            pl.BlockSpec((tm, D_MODEL), lambda i: (i, 0)),
            pl.BlockSpec((1, D_MODEL), lambda i: (0, 0)),
            pl.BlockSpec((D_MODEL, 2 * D_FF), lambda i: (0, 0), pipeline_mode=pl.Buffered(1)),
            pl.BlockSpec((D_FF, D_MODEL), lambda i: (0, 0), pipeline_mode=pl.Buffered(1)),
        ],
        out_specs=pl.BlockSpec((tm, D_MODEL), lambda i: (i, 0)),
        out_shape=jax.ShapeDtypeStruct((t, D_MODEL), F32),
        scratch_shapes=[pltpu.VMEM((tm, D_MODEL), BF16)],
        compiler_params=_cparams("parallel"),
        name="ffn",
    )(x, g.reshape(1, D_MODEL), w_in, w_out)


def _glu_kernel(x_ref, g_ref, w_ref, b_ref, o_ref):
    h = _rms(x_ref[...], g_ref[...]).astype(BF16)
    ag = jnp.dot(h, w_ref[...], preferred_element_type=F32) + b_ref[...]
    a = ag[:, :D_MODEL]
    gate = ag[:, D_MODEL:]
    o_ref[...] = a * jax.nn.sigmoid(gate)


def glu(x, g, w, b, *, tm):
    t = x.shape[0]
    return pl.pallas_call(
        _glu_kernel,
        grid=(t // tm,),
        in_specs=[
            pl.BlockSpec((tm, D_MODEL), lambda i: (i, 0)),
            pl.BlockSpec((1, D_MODEL), lambda i: (0, 0)),
            pl.BlockSpec((D_MODEL, 2 * D_MODEL), lambda i: (0, 0)),
            pl.BlockSpec((1, 2 * D_MODEL), lambda i: (0, 0)),
        ],
        out_specs=pl.BlockSpec((tm, D_MODEL), lambda i: (i, 0)),
        out_shape=jax.ShapeDtypeStruct((t, D_MODEL), F32),
        compiler_params=_cparams("parallel"),
        name="glu",
    )(x, g.reshape(1, D_MODEL), w, b.reshape(1, 2 * D_MODEL))


def _dwconv_kernel(prev_ref, v_ref, wdw_ref, bdw_ref, lng_ref, lnb_ref, y_ref, state_ref, buf_ref,
                   *, ts, valid):
    @pl.when(pl.program_id(1) == 0)
    def _():
        buf_ref[0:CONV_HALO, :] = prev_ref[...]

    buf_ref[CONV_HALO:CONV_HALO + ts, :] = v_ref[...]
    for c in range(D_MODEL // LANES):
        lanes = slice(c * LANES, (c + 1) * LANES)
        acc = jnp.zeros((ts, LANES), F32)
        for r in range(SUBLANES):
            taps = range(r, CONV_W, SUBLANES)
            xr = buf_ref[CONV_PAD + r:CONV_PAD + taps[-1] + ts, lanes]
            for w in taps:
                acc = acc + xr[w - r:w - r + ts, :] * wdw_ref[w:w + 1, lanes]
        y_ref[:, lanes] = acc + bdw_ref[:, lanes]
    y = y_ref[...]
    mu = jnp.mean(y, axis=-1, keepdims=True)
    var = jnp.mean(jnp.square(y - mu), axis=-1, keepdims=True)
    yn = (y - mu) * lax.rsqrt(var + EPS) * lng_ref[...] + lnb_ref[...]
    y_ref[...] = yn * jax.nn.sigmoid(yn)
    state_ref[...] = buf_ref[valid:valid + CONV_HALO, :]
    buf_ref[0:CONV_HALO, :] = buf_ref[ts:ts + CONV_HALO, :]


def dwconv(prev, v, w_dw, b_dw, ln_g, ln_b, *, ts, valid):
    b, s, _ = v.shape
    row = lambda a: a.reshape(1, D_MODEL)
    return pl.pallas_call(
        functools.partial(_dwconv_kernel, ts=ts, valid=valid),
        grid=(b, s // ts),
        in_specs=[
            pl.BlockSpec((None, CONV_HALO, D_MODEL), lambda i, j: (i, 0, 0)),
            pl.BlockSpec((None, ts, D_MODEL), lambda i, j: (i, j, 0)),
            pl.BlockSpec((CONV_W, D_MODEL), lambda i, j: (0, 0)),
            pl.BlockSpec((1, D_MODEL), lambda i, j: (0, 0)),
            pl.BlockSpec((1, D_MODEL), lambda i, j: (0, 0)),
            pl.BlockSpec((1, D_MODEL), lambda i, j: (0, 0)),
        ],
        out_specs=[
            pl.BlockSpec((None, ts, D_MODEL), lambda i, j: (i, j, 0)),
            pl.BlockSpec((None, CONV_HALO, D_MODEL), lambda i, j: (i, 0, 0)),
        ],
        out_shape=[
            jax.ShapeDtypeStruct((b, s, D_MODEL), F32),
            jax.ShapeDtypeStruct((b, CONV_HALO, D_MODEL), F32),
        ],
        scratch_shapes=[pltpu.VMEM((CONV_HALO + ts, D_MODEL), F32)],
        compiler_params=_cparams("parallel", "arbitrary"),
        name="dwconv",
    )(prev, v, w_dw, row(b_dw), row(ln_g), row(ln_b))


def _linear_res_kernel(y_ref, w_ref, b_ref, x_ref, o_ref, *, transposed):
    y = y_ref[...]
    if transposed:
        y = y.T
    o_ref[...] = x_ref[...] + jnp.dot(y.astype(BF16), w_ref[...], preferred_element_type=F32) + b_ref[...]


def linear_res(y, w, b, x, *, tm, transposed=False):
    t = x.shape[0]
    if transposed:
        tiles = y.shape[2] // tm
        y_spec = pl.BlockSpec((None, D_MODEL, tm), lambda i: (i // tiles, 0, i % tiles))
    else:
        y_spec = pl.BlockSpec((tm, D_MODEL), lambda i: (i, 0))
    return pl.pallas_call(
        functools.partial(_linear_res_kernel, transposed=transposed),
        grid=(t // tm,),
        in_specs=[
            y_spec,
            pl.BlockSpec((D_MODEL, D_MODEL), lambda i: (0, 0)),
            pl.BlockSpec((1, D_MODEL), lambda i: (0, 0)),
            pl.BlockSpec((tm, D_MODEL), lambda i: (i, 0)),
        ],
        out_specs=pl.BlockSpec((tm, D_MODEL), lambda i: (i, 0)),
        out_shape=jax.ShapeDtypeStruct((t, D_MODEL), F32),
        compiler_params=_cparams("parallel"),
        name="linear_res",
    )(y, w, b.reshape(1, D_MODEL), x)


def _kv_kernel(x_ref, g_ref, w_ref, kgain_ref, kt_ref, vt_ref, kn_ref, kmt_ref, knt_ref):
    j = pl.program_id(1)
    h = _rms(x_ref[...], g_ref[...]).astype(BF16)
    kv = jnp.dot(h, w_ref[...], preferred_element_type=F32)
    kt = kv[:, :D_MODEL].T
    vt = kv[:, D_MODEL:].T
    for hd in range(N_HEADS):
        rows = slice(hd * HEAD_DIM, (hd + 1) * HEAD_DIM)
        kh = kt[rows, :]
        ms = jnp.mean(kh * kh, axis=0, keepdims=True)
        khn = kh * lax.rsqrt(ms + EPS) * kgain_ref[...]
        knt_ref[rows, :] = khn
        for p in range(PAGES_PER_BLOCK):
            toks = slice(p * PAGE_SIZE, (p + 1) * PAGE_SIZE)
            kt_ref[p, hd] = khn[:, toks]
            vt_ref[p, hd] = vt[rows, toks]
    knt = knt_ref[...]
    kn_ref[...] = knt.T.astype(BF16)
    kmean = jnp.sum(knt, axis=1, keepdims=True) * (1.0 / MOBA_BLOCK)

    @pl.when(j == 0)
    def _():
        kmt_ref[...] = jnp.zeros_like(kmt_ref)

    col = lax.broadcasted_iota(jnp.int32, kmt_ref.shape, 1)
    kmt_ref[...] = jnp.where(col == j, kmean, kmt_ref[...])


def shared_kv(x, g, w, k_gain):
    b, s, _ = x.shape
    n_blk = s // MOBA_BLOCK
    pages = (b, s // PAGE_SIZE, N_HEADS, HEAD_DIM, PAGE_SIZE)
    page_spec = pl.BlockSpec((None, PAGES_PER_BLOCK, N_HEADS, HEAD_DIM, PAGE_SIZE),
                             lambda i, j: (i, j, 0, 0, 0))
    return pl.pallas_call(
        _kv_kernel,
        grid=(b, n_blk),
        in_specs=[
            pl.BlockSpec((None, MOBA_BLOCK, D_MODEL), lambda i, j: (i, j, 0)),
            pl.BlockSpec((1, D_MODEL), lambda i, j: (0, 0)),
            pl.BlockSpec((D_MODEL, 2 * D_MODEL), lambda i, j: (0, 0)),
            pl.BlockSpec((HEAD_DIM, 1), lambda i, j: (0, 0)),
        ],
        out_specs=[
            page_spec,
            page_spec,
            pl.BlockSpec((None, MOBA_BLOCK, D_MODEL), lambda i, j: (i, j, 0)),
            pl.BlockSpec((None, D_MODEL, n_blk), lambda i, j: (i, 0, 0)),
        ],
        out_shape=[
            jax.ShapeDtypeStruct(pages, F32),
            jax.ShapeDtypeStruct(pages, F32),
            jax.ShapeDtypeStruct((b, s, D_MODEL), BF16),
            jax.ShapeDtypeStruct((b, D_MODEL, n_blk), F32),
        ],
        scratch_shapes=[pltpu.VMEM((D_MODEL, MOBA_BLOCK), F32)],
        compiler_params=_cparams("parallel", "arbitrary"),
        name="shared_kv",
    )(x, g.reshape(1, D_MODEL), w, k_gain.reshape(HEAD_DIM, 1))


def _q_kernel(x_ref, g_ref, w_ref, qgain_ref, qt_ref):
    h = _rms(x_ref[...], g_ref[...]).astype(BF16)
    qt = jnp.dot(h, w_ref[...], preferred_element_type=F32).T
    for hd in range(N_HEADS):
        rows = slice(hd * HEAD_DIM, (hd + 1) * HEAD_DIM)
        qh = qt[rows, :]
        ms = jnp.mean(qh * qh, axis=0, keepdims=True)
        qt_ref[rows, :] = qh * lax.rsqrt(ms + EPS) * (qgain_ref[...] * SCALE)


def q_proj(x, g, w, q_gain, *, tm):
    b, s, _ = x.shape
    return pl.pallas_call(
        _q_kernel,
        grid=(b, s // tm),
        in_specs=[
            pl.BlockSpec((None, tm, D_MODEL), lambda i, j: (i, j, 0)),
            pl.BlockSpec((1, D_MODEL), lambda i, j: (0, 0)),
            pl.BlockSpec((D_MODEL, D_MODEL), lambda i, j: (0, 0)),
            pl.BlockSpec((HEAD_DIM, 1), lambda i, j: (0, 0)),
        ],
        out_specs=pl.BlockSpec((None, D_MODEL, tm), lambda i, j: (i, 0, j)),
        out_shape=jax.ShapeDtypeStruct((b, D_MODEL, s), F32),
        compiler_params=_cparams("parallel", "parallel"),
        name="q_proj",
    )(x, g.reshape(1, D_MODEL), w, q_gain.reshape(HEAD_DIM, 1))


FAR_GROUP = 4
LOG2E = math.log2(math.e)
ONES_ROWS = 16


def _moba_prompt_kernel(tab_ref, qt_ref, kn_ref, vt_ref, kmean_ref, o_ref, bias_ref, sel_ref, near_ref,
                        far0_ref, far1_ref, *, n_blk):
    h = pl.program_id(0)
    blk = MOBA_BLOCK
    kk = lax.broadcasted_iota(jnp.int32, (blk, blk), 0)
    qq = lax.broadcasted_iota(jnp.int32, (blk, blk), 1)

    @pl.when(pl.program_id(1) == 0)
    def _():
        for d in range(FAR_BLOCKS):
            bias_ref[d] = _bias_from_bucket(_t5_bucket(d * blk + qq - kk), tab_ref, h) * LOG2E

    far_bias = tab_ref[h, N_BUCKETS - 1] * LOG2E
    blk_f = lax.broadcasted_iota(jnp.int32, (n_blk, blk), 0).astype(F32)
    zeros_q = jnp.zeros((HEAD_DIM, blk), F32)
    ones_rows = jnp.ones((ONES_ROWS, blk), BF16)

    def q_block(c, carry):
        q0 = pl.multiple_of(c * blk, blk)
        qt = qt_ref[:, pl.ds(q0, blk)]
        ql = qt * LOG2E
        q2 = jnp.where(h % 2 == 0, jnp.concatenate([ql, zeros_q], axis=0),
                       jnp.concatenate([zeros_q, ql], axis=0)).astype(BF16)
        gate = jnp.dot(kmean_ref[...], qt, preferred_element_type=F32, precision=lax.Precision.HIGHEST)
        past = blk_f < c.astype(F32)
        g = jnp.where(past, gate, NEG_INF)
        addmask = jnp.full((n_blk, blk), NEG_INF, F32)
        for _ in range(TOP_K):
            top = jnp.max(g, axis=0, keepdims=True)
            first = jnp.min(jnp.where(g == top, blk_f, float(n_blk)), axis=0, keepdims=True)
            pick = blk_f == first
            addmask = jnp.where(pick, jnp.where(past, 0.0, NEG_INF), addmask)
            g = jnp.where(pick, NEG_INF, g)
        sel_ref[...] = addmask

        def scores(kb):
            keys = kn_ref[pl.ds(pl.multiple_of(kb * blk, blk), blk), :]
            return jnp.dot(keys, q2, preferred_element_type=F32)

        def values(kb):
            pages = [vt_ref[kb * PAGES_PER_BLOCK + p] for p in range(PAGES_PER_BLOCK)]
            return jnp.concatenate(pages, axis=1).astype(BF16)

        def attend(ma, s_ref, rows, kbs):
            m, acc = ma
            tops = [jnp.max(s_ref[i], axis=0, keepdims=True) for i in range(len(kbs))]
            tops = [t if row is None else t + row for t, row in zip(tops, rows)]
            m_new = functools.reduce(jnp.maximum, tops, m)
            ps = [jnp.exp2(s_ref[i] - (m_new if row is None else m_new - row)).astype(BF16)
                  for i, row in enumerate(rows)]
            vals = jnp.concatenate([values(kb) for kb in kbs], axis=1)
            vals = jnp.concatenate([vals, jnp.ones((ONES_ROWS, vals.shape[1]), BF16)], axis=0)
            pv = jnp.dot(vals, jnp.concatenate(ps, axis=0), preferred_element_type=F32)
            return m_new, jnp.exp2(m - m_new) * acc + pv

        n_far = jnp.maximum(c - (FAR_BLOCKS - 1), 0)

        def far_scores(g, s_ref):
            for j in range(FAR_GROUP):
                s_ref[j] = scores(jnp.minimum(g * FAR_GROUP + j, n_blk - 1))

        def far_attend(g, s_ref, ma):
            kbs = [g * FAR_GROUP + j for j in range(FAR_GROUP)]
            kbc = [jnp.minimum(kb, n_blk - 1) for kb in kbs]
            rows = [jnp.where(kb < n_far, sel_ref[pl.ds(k, 1), :] + far_bias, NEG_INF)
                    for kb, k in zip(kbs, kbc)]
            return attend(ma, s_ref, rows, kbc)

        far_scores(0, far0_ref)
        near_ref[0] = jnp.where(kk <= qq, scores(c) + bias_ref[0], NEG_INF)
        rows, kbs = [None], [c]
        for d in range(1, FAR_BLOCKS):
            kb = jnp.maximum(c - d, 0)
            near_ref[d] = scores(kb) + bias_ref[d]
            rows.append(jnp.where(c - d >= 0, sel_ref[pl.ds(kb, 1), :], NEG_INF))
            kbs.append(kb)
        init = (jnp.full((1, blk), NEG_INF, F32), jnp.zeros((HEAD_DIM + ONES_ROWS, blk), F32))
        ma = attend(init, near_ref, rows, kbs)

        def far_pair(i, ma):
            far_scores(2 * i + 1, far1_ref)
            ma = far_attend(2 * i, far0_ref, ma)
            far_scores(2 * i + 2, far0_ref)
            return far_attend(2 * i + 1, far1_ref, ma)

        n_groups = (n_far + FAR_GROUP - 1) // FAR_GROUP
        _, acc = lax.fori_loop(0, (n_groups + 1) // 2, far_pair, ma)
        o_ref[:, pl.ds(q0, blk)] = acc[:HEAD_DIM] / acc[HEAD_DIM:HEAD_DIM + 1]
        return carry

    lax.fori_loop(0, n_blk, q_block, 0)


def moba_prompt(tab, qt, kn, vt_pages, kmean):
    b, _, s = qt.shape
    n_blk = s // MOBA_BLOCK
    head_t = pl.BlockSpec((None, HEAD_DIM, s), lambda h, i: (i, h, 0))
    return pl.pallas_call(
        functools.partial(_moba_prompt_kernel, n_blk=n_blk),
        grid=(N_HEADS, b),
        in_specs=[
            pl.BlockSpec(memory_space=pltpu.SMEM),
            head_t,
            pl.BlockSpec((None, s, 2 * HEAD_DIM), lambda h, i: (i, 0, h // 2)),
            pl.BlockSpec((None, s // PAGE_SIZE, None, HEAD_DIM, PAGE_SIZE), lambda h, i: (i, 0, h, 0, 0)),
            pl.BlockSpec((None, None, n_blk, HEAD_DIM), lambda h, i: (i, h, 0, 0)),
        ],
        out_specs=head_t,
        out_shape=jax.ShapeDtypeStruct((b, D_MODEL, s), F32),
        scratch_shapes=[
            pltpu.VMEM((FAR_BLOCKS, MOBA_BLOCK, MOBA_BLOCK), F32),
            pltpu.VMEM((n_blk, MOBA_BLOCK), F32),
            pltpu.VMEM((FAR_BLOCKS, MOBA_BLOCK, MOBA_BLOCK), F32),
            pltpu.VMEM((FAR_GROUP, MOBA_BLOCK, MOBA_BLOCK), F32),
            pltpu.VMEM((FAR_GROUP, MOBA_BLOCK, MOBA_BLOCK), F32),
        ],
        compiler_params=_cparams("arbitrary", "arbitrary"),
        name="moba_prompt",
    )(tab, qt, kn, vt_pages, kmean)


PAGES_PER_STEP = 8


def _page_mean_kernel(pt_ref, *refs):
    pages, o_ref = refs[:-1], refs[-1]
    step = pl.program_id(1)
    blocks_per_step = PAGES_PER_STEP // PAGES_PER_BLOCK

    @pl.when(step == 0)
    def _():
        o_ref[...] = jnp.zeros_like(o_ref)

    col = lax.broadcasted_iota(jnp.int32, o_ref.shape, 1)
    out = o_ref[...]
    for j in range(blocks_per_step):
        tot = pages[j * PAGES_PER_BLOCK][...]
        for p in range(1, PAGES_PER_BLOCK):
            tot = tot + pages[j * PAGES_PER_BLOCK + p][...]
        mean = jnp.sum(tot, axis=-1, keepdims=True).reshape(D_MODEL, 1) * (1.0 / MOBA_BLOCK)
        out = jnp.where(col == step * blocks_per_step + j, mean, out)
    o_ref[...] = out


def page_block_means(cache_kt, page_table):
    db, n_pages = page_table.shape
    assert n_pages % PAGES_PER_STEP == 0
    n_full = n_pages // PAGES_PER_BLOCK

    def page_spec(p):
        return pl.BlockSpec((None, N_HEADS, HEAD_DIM, PAGE_SIZE),
                            lambda i, j, pt: (pt[i * n_pages + j * PAGES_PER_STEP + p], 0, 0, 0))

    return pl.pallas_call(
        _page_mean_kernel,
        grid_spec=pltpu.PrefetchScalarGridSpec(
            num_scalar_prefetch=1,
            grid=(db, n_pages // PAGES_PER_STEP),
            in_specs=[page_spec(p) for p in range(PAGES_PER_STEP)],
            out_specs=pl.BlockSpec((None, D_MODEL, n_full), lambda i, j, pt: (i, 0, 0)),
        ),
        out_shape=jax.ShapeDtypeStruct((db, D_MODEL, n_full), F32),
        compiler_params=_cparams("parallel", "arbitrary"),
        name="page_block_means",
    )(page_table.reshape(-1), *([cache_kt] * PAGES_PER_STEP))


def _gate_topk_kernel(qt_ref, kmt_ref, segt_ref, pt_ref, o_ref, *, n_full, t_new):
    kmt = kmt_ref[...]
    blk_f = lax.broadcasted_iota(jnp.int32, (N_HEADS, n_full), 1).astype(F32)
    for t in range(t_new):
        prod = kmt * qt_ref[:, t:t + 1]
        g = jnp.dot(segt_ref[...], prod, preferred_element_type=F32, precision=lax.Precision.HIGHEST)
        for j in range(TOP_K):
            top = jnp.max(g, axis=1, keepdims=True)
            first = jnp.min(jnp.where(g == top, blk_f, float(n_full)), axis=1, keepdims=True)
            pick = blk_f == first
            base = (t * TOP_K + j) * (1 + PAGES_PER_BLOCK)
            o_ref[:, base:base + 1] = first.astype(jnp.int32)
            for p in range(PAGES_PER_BLOCK):
                page = jnp.sum(jnp.where(pick, pt_ref[p:p + 1, :], 0.0), axis=1, keepdims=True)
                o_ref[:, base + 1 + p:base + 2 + p] = page.astype(jnp.int32)
            g = jnp.where(pick, NEG_INF, g)


def gate_topk(qt, kmt, page_table):
    db, _, t_new = qt.shape
    n_full = kmt.shape[2]
    width = t_new * TOP_K * (1 + PAGES_PER_BLOCK)
    segt = (jnp.arange(D_MODEL)[None, :] // HEAD_DIM == jnp.arange(N_HEADS)[:, None]).astype(F32)
    pt = page_table.reshape(db, n_full, PAGES_PER_BLOCK).transpose(0, 2, 1).astype(F32)
    out = pl.pallas_call(
        functools.partial(_gate_topk_kernel, n_full=n_full, t_new=t_new),
        grid=(db,),
        in_specs=[
            pl.BlockSpec((None, D_MODEL, t_new), lambda i: (i, 0, 0)),
            pl.BlockSpec((None, D_MODEL, n_full), lambda i: (i, 0, 0)),
            pl.BlockSpec((N_HEADS, D_MODEL), lambda i: (0, 0)),
            pl.BlockSpec((None, PAGES_PER_BLOCK, n_full), lambda i: (i, 0, 0)),
        ],
        out_specs=pl.BlockSpec((None, N_HEADS, width), lambda i: (i, 0, 0)),
        out_shape=jax.ShapeDtypeStruct((db, N_HEADS, width), jnp.int32),
        compiler_params=_cparams("parallel"),
        name="gate_topk",
    )(qt, kmt, segt, pt)
    return out.reshape(db, N_HEADS, t_new, TOP_K, 1 + PAGES_PER_BLOCK)


def _moba_sample_kernel(sel_ref, tab_ref, qt_ref, knt_ref, vnt_ref, *refs, t_new, past):
    n_sel_pages = t_new * TOP_K * PAGES_PER_BLOCK
    k_pages, v_pages, o_ref = refs[:n_sel_pages], refs[n_sel_pages:2 * n_sel_pages], refs[-1]
    b = pl.program_id(0)
    h = pl.program_id(1)
    new_id = lax.broadcasted_iota(jnp.int32, (1, t_new), 1)
    row_id = lax.broadcasted_iota(jnp.int32, (1, PAGE_SIZE), 1)
    knt = knt_ref[...]
    vnt = vnt_ref[...]
    stride = TOP_K * (1 + PAGES_PER_BLOCK)
    for t in range(t_new):
        q = qt_ref[:, t:t + 1]
        s_own = jnp.sum(knt * q, axis=0, keepdims=True)
        s_own = s_own + _bias_from_bucket(_t5_bucket(t - new_id), tab_ref, h)
        s_own = jnp.where(new_id <= t, s_own, NEG_INF)
        scores = []
        for j in range(TOP_K):
            idx = sel_ref[((b * N_HEADS + h) * t_new + t) * stride + j * (1 + PAGES_PER_BLOCK)]
            for p in range(PAGES_PER_BLOCK):
                slot = (t * TOP_K + j) * PAGES_PER_BLOCK + p
                s = jnp.sum(k_pages[slot][...] * q, axis=0, keepdims=True)
                d0 = past + t - (idx * MOBA_BLOCK + p * PAGE_SIZE)
                scores.append(s + _bias_from_bucket(_t5_bucket(d0 - row_id), tab_ref, h))
        m = jnp.max(s_own, axis=1, keepdims=True)
        for s in scores:
            m = jnp.maximum(m, jnp.max(s, axis=1, keepdims=True))
        e_own = jnp.exp(s_own - m)
        den = jnp.sum(e_own, axis=1, keepdims=True)
        num = jnp.sum(vnt * e_own, axis=1, keepdims=True)
        weighted = jnp.zeros((HEAD_DIM, PAGE_SIZE), F32)
        for i, s in enumerate(scores):
            e = jnp.exp(s - m)
            den = den + jnp.sum(e, axis=1, keepdims=True)
            weighted = weighted + v_pages[t * TOP_K * PAGES_PER_BLOCK + i][...] * e
        num = num + jnp.sum(weighted, axis=1, keepdims=True)
        o_ref[:, t:t + 1] = num / den


def moba_sample(qt, knt, vnt, cache_kt, cache_vt, sel, tab, *, past):
    db, _, _, t_new = qt.shape
    n_sel_pages = t_new * TOP_K * PAGES_PER_BLOCK
    col_spec = pl.BlockSpec((None, None, HEAD_DIM, t_new), lambda i, h, se: (i, h, 0, 0))

    def page_spec(slot):
        tj, p = divmod(slot, PAGES_PER_BLOCK)
        pos = tj * (1 + PAGES_PER_BLOCK) + 1 + p
        width = t_new * TOP_K * (1 + PAGES_PER_BLOCK)
        return pl.BlockSpec((None, None, HEAD_DIM, PAGE_SIZE),
                            lambda i, h, se: (se[(i * N_HEADS + h) * width + pos], h, 0, 0))

    pages = [page_spec(s) for s in range(n_sel_pages)]
    return pl.pallas_call(
        functools.partial(_moba_sample_kernel, t_new=t_new, past=past),
        grid_spec=pltpu.PrefetchScalarGridSpec(
            num_scalar_prefetch=1,
            grid=(db, N_HEADS),
            in_specs=[pl.BlockSpec(memory_space=pltpu.SMEM), col_spec, col_spec, col_spec] + pages + pages,
            out_specs=col_spec,
        ),
        out_shape=jax.ShapeDtypeStruct((db, N_HEADS, HEAD_DIM, t_new), F32),
        compiler_params=_cparams("parallel", "parallel"),
        name="moba_sample",
    )(sel, tab, qt, knt, vnt, *([cache_kt] * n_sel_pages), *([cache_vt] * n_sel_pages))


def _bf16_weights(w_ffn_in, w_ffn_out, w_pw1, w_pw2, w_kv, w_q, w_o):
    return tuple(w.astype(BF16) for w in (w_ffn_in, w_ffn_out, w_pw1, w_pw2, w_kv, w_q, w_o))


def kernel(x_prompt, x_sample, state_conv, cache_k, cache_v, page_table, norm_ffn, w_ffn_in, w_ffn_out,
           norm_mix, w_pw1, b_pw1, w_dw, b_dw, ln_g, ln_b, w_pw2, b_pw2, norm_kv, w_kv, k_gain, w_q, q_gain,
           w_o, rel_bias):
    bsz, seq, _ = x_prompt.shape
    dbs, t_new, _ = x_sample.shape
    n_pages = page_table.shape[1]
    past = n_pages * PAGE_SIZE
    assert past % MOBA_BLOCK == 0 and t_new <= SUBLANES and seq % MOBA_BLOCK == 0
    n_tok_s = dbs * t_new
    w_ffn_in, w_ffn_out, w_pw1, w_pw2, w_kv, w_q, w_o = _bf16_weights(
        w_ffn_in, w_ffn_out, w_pw1, w_pw2, w_kv, w_q, w_o)
    tab = rel_bias.T
    zero_bias = jnp.zeros((D_MODEL,), F32)

    xp = x_prompt.reshape(bsz * seq, D_MODEL)
    xs = x_sample.reshape(n_tok_s, D_MODEL)
    tm_p, tm_s = 512, n_tok_s

    xp = ffn(xp, norm_ffn[0, 0], w_ffn_in[0, 0], w_ffn_out[0, 0], tm=tm_p)
    xs = ffn(xs, norm_ffn[0, 0], w_ffn_in[0, 0], w_ffn_out[0, 0], tm=tm_s)

    vp = glu(xp, norm_mix[0], w_pw1[0], b_pw1[0], tm=tm_p).reshape(bsz, seq, D_MODEL)
    vs = glu(xs, norm_mix[0], w_pw1[0], b_pw1[0], tm=tm_s).reshape(dbs, t_new, D_MODEL)
    prev_p = jnp.zeros((bsz, CONV_HALO, D_MODEL), F32)
    prev_s = jnp.pad(state_conv[0], ((0, 0), (CONV_PAD, 0), (0, 0)))
    vs_pad = jnp.pad(vs, ((0, 0), (0, SUBLANES - t_new), (0, 0)))
    yp, st_p = dwconv(prev_p, vp, w_dw[0], b_dw[0], ln_g[0], ln_b[0], ts=PAGE_SIZE, valid=PAGE_SIZE)
    ys, st_s = dwconv(prev_s, vs_pad, w_dw[0], b_dw[0], ln_g[0], ln_b[0], ts=SUBLANES, valid=t_new)
    conv_prompt = st_p[None, :, CONV_PAD:]
    conv_sample = st_s[None, :, CONV_PAD:]
    xp = linear_res(yp.reshape(bsz * seq, D_MODEL), w_pw2[0], b_pw2[0], xp, tm=tm_p)
    xs = linear_res(ys[:, :t_new].reshape(n_tok_s, D_MODEL), w_pw2[0], b_pw2[0], xs, tm=tm_s)

    xp = ffn(xp, norm_ffn[0, 1], w_ffn_in[0, 1], w_ffn_out[0, 1], tm=tm_p)
    xs = ffn(xs, norm_ffn[0, 1], w_ffn_in[0, 1], w_ffn_out[0, 1], tm=tm_s)

    kt_pp, vt_pp, kn_p, kmt_p = shared_kv(xp.reshape(bsz, seq, D_MODEL), norm_kv, w_kv, k_gain)
    xs_blk = jnp.pad(xs, ((0, MOBA_BLOCK - n_tok_s), (0, 0)))[None]
    kt_sp, vt_sp, _, _ = shared_kv(xs_blk, norm_kv, w_kv, k_gain)

    def sample_cols(pages_t):
        return pages_t[0, 0, :, :, :n_tok_s].reshape(N_HEADS, HEAD_DIM, dbs, t_new).transpose(2, 0, 1, 3)

    knt_s, vnt_s = sample_cols(kt_sp), sample_cols(vt_sp)
    k_pp, v_pp = kt_pp.swapaxes(3, 4), vt_pp.swapaxes(3, 4)
    k_rows, v_rows = knt_s.swapaxes(2, 3), vnt_s.swapaxes(2, 3)
    cache_kt, cache_vt = cache_k.swapaxes(2, 3), cache_v.swapaxes(2, 3)

    xp = ffn(xp, norm_ffn[1, 0], w_ffn_in[1, 0], w_ffn_out[1, 0], tm=tm_p)
    xs = ffn(xs, norm_ffn[1, 0], w_ffn_in[1, 0], w_ffn_out[1, 0], tm=tm_s)

    qt_p = q_proj(xp.reshape(bsz, seq, D_MODEL), norm_mix[1], w_q[0], q_gain[0], tm=tm_p)
    kmean_p = kmt_p.reshape(bsz, N_HEADS, HEAD_DIM, -1).swapaxes(2, 3)
    at_p = moba_prompt(tab, qt_p, kn_p, vt_pp, kmean_p)
    xp = linear_res(at_p, w_o[0], zero_bias, xp, tm=tm_p, transposed=True)

    qt_s = q_proj(xs[None], norm_mix[1], w_q[0], q_gain[0], tm=n_tok_s)[0]
    qt_s = qt_s.reshape(D_MODEL, dbs, t_new).transpose(1, 0, 2)
    kmt_s = page_block_means(cache_kt, page_table)
    sel = gate_topk(qt_s, kmt_s, page_table)
    at_s = moba_sample(qt_s.reshape(dbs, N_HEADS, HEAD_DIM, t_new), knt_s, vnt_s, cache_kt, cache_vt,
                       sel.reshape(-1), tab, past=past)
    a_s = at_s.transpose(0, 3, 1, 2).reshape(n_tok_s, D_MODEL)
    xs = linear_res(a_s, w_o[0], zero_bias, xs, tm=tm_s)

    xp = ffn(xp, norm_ffn[1, 1], w_ffn_in[1, 1], w_ffn_out[1, 1], tm=tm_p)
    xs = ffn(xs, norm_ffn[1, 1], w_ffn_in[1, 1], w_ffn_out[1, 1], tm=tm_s)

    return (xp.reshape(bsz, seq, D_MODEL), xs.reshape(dbs, t_new, D_MODEL), conv_prompt, conv_sample,
            k_pp, v_pp, k_rows, v_rows)
```

```python
import functools
import math

import numpy as np
import jax
import jax.numpy as jnp
from jax import lax
from jax.experimental import pallas as pl
from jax.experimental.pallas import tpu as pltpu

D_MODEL = 1024
N_HEADS = 16
HEAD_DIM = D_MODEL // N_HEADS
D_FF = 2816
CONV_W = 31
MOBA_BLOCK = 256
TOP_K = 3
PAGE_SIZE = 128
PAGES_PER_BLOCK = MOBA_BLOCK // PAGE_SIZE
N_BUCKETS = 32
MAX_EXACT = N_BUCKETS // 2
REL_MAX_DIST = 2048
EPS = 1e-6
SCALE = HEAD_DIM ** -0.5

LANES = 128
SUBLANES = 8
VMEM_LIMIT = 56 * 1024 * 1024

F32 = jnp.float32
BF16 = jnp.bfloat16
NEG_INF = float("-inf")

CONV_HALO = 32
CONV_PAD = CONV_HALO - (CONV_W - 1)


def _first_far_distance():
    n = np.arange(1, 4 * REL_MAX_DIST, dtype=np.float32)
    large = MAX_EXACT + (np.log(n / np.float32(MAX_EXACT)) / np.float32(math.log(REL_MAX_DIST / MAX_EXACT))
                         * np.float32(N_BUCKETS - MAX_EXACT)).astype(np.int32)
    bucket = np.where(n < MAX_EXACT, n.astype(np.int32), np.minimum(large, N_BUCKETS - 1))
    return int(n[np.argmax(bucket == N_BUCKETS - 1)])


FAR_DIST = _first_far_distance()
FAR_BLOCKS = -(-(FAR_DIST + MOBA_BLOCK - 1) // MOBA_BLOCK)


def _cparams(*sem):
    return pltpu.CompilerParams(dimension_semantics=sem, vmem_limit_bytes=VMEM_LIMIT)


def _rms(x, g):
    ms = jnp.mean(x * x, axis=-1, keepdims=True)
    return x * lax.rsqrt(ms + EPS) * g


def _t5_bucket(delta):
    n = jnp.maximum(delta, 0)
    nf = jnp.maximum(n, 1).astype(F32)
    large = MAX_EXACT + (jnp.log(nf / MAX_EXACT) / math.log(REL_MAX_DIST / MAX_EXACT)
                         * (N_BUCKETS - MAX_EXACT)).astype(jnp.int32)
    large = jnp.minimum(large, N_BUCKETS - 1)
    return jnp.where(n < MAX_EXACT, n, large)


def _bias_from_bucket(bucket, tab_ref, h):
    val = jnp.zeros(bucket.shape, F32)
    for bb in range(N_BUCKETS):
        val = jnp.where(bucket == bb, tab_ref[h, bb], val)
    return val


def _ffn_kernel(x_ref, g_ref, win_ref, wout_ref, o_ref, h_ref, *, tf):
    x = x_ref[...]
    h_ref[...] = _rms(x, g_ref[...]).astype(BF16)
    acc = None
    for j in range(D_FF // tf):
        h = h_ref[...]
        a = jnp.dot(h, win_ref[:, j * tf:(j + 1) * tf], preferred_element_type=F32)
        g = jnp.dot(h, win_ref[:, D_FF + j * tf:D_FF + (j + 1) * tf], preferred_element_type=F32)
        act = (a * jax.nn.sigmoid(a) * g).astype(BF16)
        part = jnp.dot(act, wout_ref[j * tf:(j + 1) * tf, :], preferred_element_type=F32)
        acc = part if acc is None else acc + part
    o_ref[...] = x + 0.5 * acc


def ffn(x, g, w_in, w_out, *, tm, tf=D_FF // 2):
    t = x.shape[0]
    return pl.pallas_call(
        functools.partial(_ffn_kernel, tf=tf),
        grid=(t // tm,),
        in_specs=[
            pl.BlockSpec((tm, D_MODEL), lambda i: (i, 0)),
            pl.BlockSpec((1, D_MODEL), lambda i: (0, 0)),
            pl.BlockSpec((D_MODEL, 2 * D_FF), lambda i: (0, 0), pipeline_mode=pl.Buffered(1)),
            pl.BlockSpec((D_FF, D_MODEL), lambda i: (0, 0), pipeline_mode=pl.Buffered(1)),
        ],
        out_specs=pl.BlockSpec((tm, D_MODEL), lambda i: (i, 0)),
        out_shape=jax.ShapeDtypeStruct((t, D_MODEL), F32),
        scratch_shapes=[pltpu.VMEM((tm, D_MODEL), BF16)],
        compiler_params=_cparams("parallel"),
        name="ffn",
    )(x, g.reshape(1, D_MODEL), w_in, w_out)


def _glu_kernel(x_ref, g_ref, w_ref, b_ref, o_ref):
    h = _rms(x_ref[...], g_ref[...]).astype(BF16)
    ag = jnp.dot(h, w_ref[...], preferred_element_type=F32) + b_ref[...]
    a = ag[:, :D_MODEL]
    gate = ag[:, D_MODEL:]
    o_ref[...] = a * jax.nn.sigmoid(gate)


def glu(x, g, w, b, *, tm):
    t = x.shape[0]
    return pl.pallas_call(
        _glu_kernel,
        grid=(t // tm,),
        in_specs=[
            pl.BlockSpec((tm, D_MODEL), lambda i: (i, 0)),
            pl.BlockSpec((1, D_MODEL), lambda i: (0, 0)),
            pl.BlockSpec((D_MODEL, 2 * D_MODEL), lambda i: (0, 0)),
            pl.BlockSpec((1, 2 * D_MODEL), lambda i: (0, 0)),
        ],
        out_specs=pl.BlockSpec((tm, D_MODEL), lambda i: (i, 0)),
        out_shape=jax.ShapeDtypeStruct((t, D_MODEL), F32),
        compiler_params=_cparams("parallel"),
        name="glu",
    )(x, g.reshape(1, D_MODEL), w, b.reshape(1, 2 * D_MODEL))


def _dwconv_kernel(prev_ref, v_ref, wdw_ref, bdw_ref, lng_ref, lnb_ref, y_ref, state_ref, buf_ref,
                   shift_ref, *, ts, valid):
    @pl.when(pl.program_id(1) == 0)
    def _():
        buf_ref[0:CONV_HALO, :] = prev_ref[...]

    buf_ref[CONV_HALO:CONV_HALO + ts, :] = v_ref[...]
    for c in range(D_MODEL // LANES):
        lanes = slice(c * LANES, (c + 1) * LANES)
        acc = jnp.zeros((ts, LANES), F32)
        for r in range(SUBLANES):
            taps = range(r, CONV_W, SUBLANES)
            span = taps[-1] - r + ts
            shift_ref[r % 2, 0:span, :] = buf_ref[CONV_PAD + r:CONV_PAD + r + span, lanes]
            for w in taps:
                acc = acc + shift_ref[r % 2, w - r:w - r + ts, :] * wdw_ref[w:w + 1, lanes]
        y_ref[:, lanes] = acc + bdw_ref[:, lanes]
    y = y_ref[...]
    mu = jnp.mean(y, axis=-1, keepdims=True)
    var = jnp.mean(jnp.square(y - mu), axis=-1, keepdims=True)
    yn = (y - mu) * lax.rsqrt(var + EPS) * lng_ref[...] + lnb_ref[...]
    y_ref[...] = yn * jax.nn.sigmoid(yn)
    state_ref[...] = buf_ref[valid:valid + CONV_HALO, :]
    buf_ref[0:CONV_HALO, :] = buf_ref[ts:ts + CONV_HALO, :]


def dwconv(prev, v, w_dw, b_dw, ln_g, ln_b, *, ts, valid):
    b, s, _ = v.shape
    row = lambda a: a.reshape(1, D_MODEL)
    return pl.pallas_call(
        functools.partial(_dwconv_kernel, ts=ts, valid=valid),
        grid=(b, s // ts),
        in_specs=[
            pl.BlockSpec((None, CONV_HALO, D_MODEL), lambda i, j: (i, 0, 0)),
            pl.BlockSpec((None, ts, D_MODEL), lambda i, j: (i, j, 0)),
            pl.BlockSpec((CONV_W, D_MODEL), lambda i, j: (0, 0)),
            pl.BlockSpec((1, D_MODEL), lambda i, j: (0, 0)),
            pl.BlockSpec((1, D_MODEL), lambda i, j: (0, 0)),
            pl.BlockSpec((1, D_MODEL), lambda i, j: (0, 0)),
        ],
        out_specs=[
            pl.BlockSpec((None, ts, D_MODEL), lambda i, j: (i, j, 0)),
            pl.BlockSpec((None, CONV_HALO, D_MODEL), lambda i, j: (i, 0, 0)),
        ],
        out_shape=[
            jax.ShapeDtypeStruct((b, s, D_MODEL), F32),
            jax.ShapeDtypeStruct((b, CONV_HALO, D_MODEL), F32),
        ],
        scratch_shapes=[pltpu.VMEM((CONV_HALO + ts, D_MODEL), F32),
                        pltpu.VMEM((2, CONV_HALO + ts, LANES), F32)],
        compiler_params=_cparams("parallel", "arbitrary"),
        name="dwconv",
    )(prev, v, w_dw, row(b_dw), row(ln_g), row(ln_b))


def _linear_res_kernel(y_ref, w_ref, b_ref, x_ref, o_ref, *, transposed):
    y = y_ref[...]
    if transposed:
        y = y.T
    o_ref[...] = x_ref[...] + jnp.dot(y.astype(BF16), w_ref[...], preferred_element_type=F32) + b_ref[...]


def linear_res(y, w, b, x, *, tm, transposed=False):
    t = x.shape[0]
    if transposed:
        tiles = y.shape[2] // tm
        y_spec = pl.BlockSpec((None, D_MODEL, tm), lambda i: (i // tiles, 0, i % tiles))
    else:
        y_spec = pl.BlockSpec((tm, D_MODEL), lambda i: (i, 0))
    return pl.pallas_call(
        functools.partial(_linear_res_kernel, transposed=transposed),
        grid=(t // tm,),
        in_specs=[
            y_spec,
            pl.BlockSpec((D_MODEL, D_MODEL), lambda i: (0, 0)),
            pl.BlockSpec((1, D_MODEL), lambda i: (0, 0)),
            pl.BlockSpec((tm, D_MODEL), lambda i: (i, 0)),
        ],
        out_specs=pl.BlockSpec((tm, D_MODEL), lambda i: (i, 0)),
        out_shape=jax.ShapeDtypeStruct((t, D_MODEL), F32),
        compiler_params=_cparams("parallel"),
        name="linear_res",
    )(y, w, b.reshape(1, D_MODEL), x)


def _kv_kernel(x_ref, g_ref, w_ref, kgain_ref, kt_ref, vt_ref, kn_ref, kmt_ref, knt_ref):
    j = pl.program_id(1)
    h = _rms(x_ref[...], g_ref[...]).astype(BF16)
    kv = jnp.dot(h, w_ref[...], preferred_element_type=F32)
    kt = kv[:, :D_MODEL].T
    vt = kv[:, D_MODEL:].T
    for hd in range(N_HEADS):
        rows = slice(hd * HEAD_DIM, (hd + 1) * HEAD_DIM)
        kh = kt[rows, :]
        ms = jnp.mean(kh * kh, axis=0, keepdims=True)
        khn = kh * lax.rsqrt(ms + EPS) * kgain_ref[...]
        knt_ref[rows, :] = khn
        for p in range(PAGES_PER_BLOCK):
            toks = slice(p * PAGE_SIZE, (p + 1) * PAGE_SIZE)
            kt_ref[p, hd] = khn[:, toks]
            vt_ref[p, hd] = vt[rows, toks]
    knt = knt_ref[...]
    kn_ref[...] = knt.T.astype(BF16)
    kmean = jnp.sum(knt, axis=1, keepdims=True) * (1.0 / MOBA_BLOCK)

    @pl.when(j == 0)
    def _():
        kmt_ref[...] = jnp.zeros_like(kmt_ref)

    col = lax.broadcasted_iota(jnp.int32, kmt_ref.shape, 1)
    kmt_ref[...] = jnp.where(col == j, kmean, kmt_ref[...])


def shared_kv(x, g, w, k_gain):
    b, s, _ = x.shape
    n_blk = s // MOBA_BLOCK
    pages = (b, s // PAGE_SIZE, N_HEADS, HEAD_DIM, PAGE_SIZE)
    page_spec = pl.BlockSpec((None, PAGES_PER_BLOCK, N_HEADS, HEAD_DIM, PAGE_SIZE),
                             lambda i, j: (i, j, 0, 0, 0))
    return pl.pallas_call(
        _kv_kernel,
        grid=(b, n_blk),
        in_specs=[
            pl.BlockSpec((None, MOBA_BLOCK, D_MODEL), lambda i, j: (i, j, 0)),
            pl.BlockSpec((1, D_MODEL), lambda i, j: (0, 0)),
            pl.BlockSpec((D_MODEL, 2 * D_MODEL), lambda i, j: (0, 0)),
            pl.BlockSpec((HEAD_DIM, 1), lambda i, j: (0, 0)),
        ],
        out_specs=[
            page_spec,
            page_spec,
            pl.BlockSpec((None, MOBA_BLOCK, D_MODEL), lambda i, j: (i, j, 0)),
            pl.BlockSpec((None, D_MODEL, n_blk), lambda i, j: (i, 0, 0)),
        ],
        out_shape=[
            jax.ShapeDtypeStruct(pages, F32),
            jax.ShapeDtypeStruct(pages, F32),
            jax.ShapeDtypeStruct((b, s, D_MODEL), BF16),
            jax.ShapeDtypeStruct((b, D_MODEL, n_blk), F32),
        ],
        scratch_shapes=[pltpu.VMEM((D_MODEL, MOBA_BLOCK), F32)],
        compiler_params=_cparams("parallel", "arbitrary"),
        name="shared_kv",
    )(x, g.reshape(1, D_MODEL), w, k_gain.reshape(HEAD_DIM, 1))


def _q_kernel(x_ref, g_ref, w_ref, qgain_ref, qt_ref):
    h = _rms(x_ref[...], g_ref[...]).astype(BF16)
    qt = jnp.dot(h, w_ref[...], preferred_element_type=F32).T
    for hd in range(N_HEADS):
        rows = slice(hd * HEAD_DIM, (hd + 1) * HEAD_DIM)
        qh = qt[rows, :]
        ms = jnp.mean(qh * qh, axis=0, keepdims=True)
        qt_ref[rows, :] = qh * lax.rsqrt(ms + EPS) * (qgain_ref[...] * SCALE)


def q_proj(x, g, w, q_gain, *, tm):
    b, s, _ = x.shape
    return pl.pallas_call(
        _q_kernel,
        grid=(b, s // tm),
        in_specs=[
            pl.BlockSpec((None, tm, D_MODEL), lambda i, j: (i, j, 0)),
            pl.BlockSpec((1, D_MODEL), lambda i, j: (0, 0)),
            pl.BlockSpec((D_MODEL, D_MODEL), lambda i, j: (0, 0)),
            pl.BlockSpec((HEAD_DIM, 1), lambda i, j: (0, 0)),
        ],
        out_specs=pl.BlockSpec((None, D_MODEL, tm), lambda i, j: (i, 0, j)),
        out_shape=jax.ShapeDtypeStruct((b, D_MODEL, s), F32),
        compiler_params=_cparams("parallel", "parallel"),
        name="q_proj",
    )(x, g.reshape(1, D_MODEL), w, q_gain.reshape(HEAD_DIM, 1))


FAR_GROUP = 4
NEAR_SPLIT = 4
HEAD_PAIR = 2
LOG2E = math.log2(math.e)
ONES_ROWS = 16


def _moba_prompt_kernel(tab_ref, qt_ref, kn_ref, vt_ref, kmean_ref, o_ref, bias_ref, sel_ref, near_ref,
                        far0_ref, far1_ref, *, n_blk):
    hp = pl.program_id(0)
    blk = MOBA_BLOCK
    wide = HEAD_PAIR * blk
    kk = lax.broadcasted_iota(jnp.int32, (blk, blk), 0)
    qq = lax.broadcasted_iota(jnp.int32, (blk, blk), 1)

    @pl.when(pl.program_id(1) == 0)
    def _():
        for d in range(FAR_BLOCKS):
            bucket = _t5_bucket(d * blk + qq - kk)
            for e in range(HEAD_PAIR):
                bias_ref[d, :, e * blk:(e + 1) * blk] = (
                    _bias_from_bucket(bucket, tab_ref, HEAD_PAIR * hp + e) * LOG2E)

    far_bias = jnp.concatenate(
        [jnp.full((1, blk), tab_ref[HEAD_PAIR * hp + e, N_BUCKETS - 1] * LOG2E, F32) for e in range(HEAD_PAIR)],
        axis=1)
    causal = jnp.concatenate([kk <= qq] * HEAD_PAIR, axis=1)
    blk_f = lax.broadcasted_iota(jnp.int32, (n_blk, wide), 0).astype(F32)
    zeros_q = jnp.zeros((HEAD_DIM, blk), F32)

    def q_block(c, carry):
        q0 = pl.multiple_of(c * blk, blk)
        qt = qt_ref[:, pl.ds(q0, blk)]
        qs = [qt[e * HEAD_DIM:(e + 1) * HEAD_DIM] for e in range(HEAD_PAIR)]
        q2 = jnp.concatenate(
            [jnp.concatenate([qs[0] * LOG2E, zeros_q], axis=0), jnp.concatenate([zeros_q, qs[1] * LOG2E], axis=0)],
            axis=1).astype(BF16)
        gate = jnp.concatenate(
            [jnp.dot(kmean_ref[e], qs[e], preferred_element_type=F32, precision=lax.Precision.HIGHEST)
             for e in range(HEAD_PAIR)], axis=1)
        past = blk_f < lax.convert_element_type(c, F32)
        g = jnp.where(past, gate, NEG_INF)
        addmask = jnp.full((n_blk, wide), NEG_INF, F32)
        for _ in range(TOP_K):
            top = jnp.max(g, axis=0, keepdims=True)
            first = jnp.min(jnp.where(g == top, blk_f, float(n_blk)), axis=0, keepdims=True)
            pick = blk_f == first
            addmask = jnp.where(pick, jnp.where(past, 0.0, NEG_INF), addmask)
            g = jnp.where(pick, NEG_INF, g)
        sel_ref[...] = addmask

        def scores(kb):
            keys = kn_ref[pl.ds(pl.multiple_of(kb * blk, blk), blk), :]
            return jnp.dot(keys, q2, preferred_element_type=F32)

        def values(kb, e):
            pages = [vt_ref[kb * PAGES_PER_BLOCK + p, e] for p in range(PAGES_PER_BLOCK)]
            return jnp.concatenate(pages, axis=1).astype(BF16)

        def attend(ma, s_ref, rows, kbs, first=0):
            m, acc = ma
            tops = [jnp.max(s_ref[first + i], axis=0, keepdims=True) for i in range(len(kbs))]
            tops = [t if row is None else t + row for t, row in zip(tops, rows)]
            m_new = functools.reduce(jnp.maximum, tops, m)
            ps = jnp.concatenate(
                [jnp.exp2(s_ref[first + i] - (m_new if row is None else m_new - row)).astype(BF16)
                 for i, row in enumerate(rows)], axis=0)
            ones = jnp.ones((ONES_ROWS, ps.shape[0]), BF16)
            pvs = []
            for e in range(HEAD_PAIR):
                vals = jnp.concatenate([values(kb, e) for kb in kbs], axis=1)
                pvs.append(jnp.dot(jnp.concatenate([vals, ones], axis=0), ps[:, e * blk:(e + 1) * blk],
                                   preferred_element_type=F32))
            return m_new, jnp.exp2(m - m_new) * acc + jnp.concatenate(pvs, axis=1)

        n_far = jnp.maximum(c - (FAR_BLOCKS - 1), 0)

        def far_scores(g, s_ref):
            for j in range(FAR_GROUP):
                s_ref[j] = scores(jnp.minimum(g * FAR_GROUP + j, n_blk - 1))

        def far_attend(g, s_ref, ma):
            kbs = [g * FAR_GROUP + j for j in range(FAR_GROUP)]
            kbc = [jnp.minimum(kb, n_blk - 1) for kb in kbs]
            rows = [jnp.where(kb < n_far, sel_ref[pl.ds(k, 1), :] + far_bias, NEG_INF)
                    for kb, k in zip(kbs, kbc)]
            return attend(ma, s_ref, rows, kbc)

        far_scores(0, far0_ref)
        near_ref[0] = jnp.where(causal, scores(c) + bias_ref[0], NEG_INF)
        rows, kbs = [None], [c]
        for d in range(1, FAR_BLOCKS):
            kb = jnp.maximum(c - d, 0)
            near_ref[d] = scores(kb) + bias_ref[d]
            rows.append(jnp.where(c - d >= 0, sel_ref[pl.ds(kb, 1), :], NEG_INF))
            kbs.append(kb)
        init = (jnp.full((1, wide), NEG_INF, F32), jnp.zeros((HEAD_DIM + ONES_ROWS, wide), F32))
        ma = attend(init, near_ref, rows[:NEAR_SPLIT], kbs[:NEAR_SPLIT])
        ma = attend(ma, near_ref, rows[NEAR_SPLIT:], kbs[NEAR_SPLIT:], first=NEAR_SPLIT)

        def far_pair(i, ma):
            far_scores(2 * i + 1, far1_ref)
            ma = far_attend(2 * i, far0_ref, ma)
            far_scores(2 * i + 2, far0_ref)
            return far_attend(2 * i + 1, far1_ref, ma)

        n_groups = (n_far + FAR_GROUP - 1) // FAR_GROUP
        _, acc = lax.fori_loop(0, (n_groups + 1) // 2, far_pair, ma)
        for e in range(HEAD_PAIR):
            cols = slice(e * blk, (e + 1) * blk)
            o_ref[e * HEAD_DIM:(e + 1) * HEAD_DIM, pl.ds(q0, blk)] = (
                acc[:HEAD_DIM, cols] / acc[HEAD_DIM:HEAD_DIM + 1, cols])
        return carry

    lax.fori_loop(0, n_blk, q_block, 0)


def moba_prompt(tab, qt, kn, vt_pages, kmean):
    b, _, s = qt.shape
    n_blk = s // MOBA_BLOCK
    wide = HEAD_PAIR * MOBA_BLOCK
    pair_t = pl.BlockSpec((None, HEAD_PAIR * HEAD_DIM, s), lambda hp, i: (i, hp, 0))
    return pl.pallas_call(
        functools.partial(_moba_prompt_kernel, n_blk=n_blk),
        grid=(N_HEADS // HEAD_PAIR, b),
        in_specs=[
            pl.BlockSpec(memory_space=pltpu.SMEM),
            pair_t,
            pl.BlockSpec((None, s, HEAD_PAIR * HEAD_DIM), lambda hp, i: (i, 0, hp)),
            pl.BlockSpec((None, s // PAGE_SIZE, HEAD_PAIR, HEAD_DIM, PAGE_SIZE), lambda hp, i: (i, 0, hp, 0, 0)),
            pl.BlockSpec((None, HEAD_PAIR, n_blk, HEAD_DIM), lambda hp, i: (i, hp, 0, 0)),
        ],
        out_specs=pair_t,
        out_shape=jax.ShapeDtypeStruct((b, D_MODEL, s), F32),
        scratch_shapes=[
            pltpu.VMEM((FAR_BLOCKS, MOBA_BLOCK, wide), F32),
            pltpu.VMEM((n_blk, wide), F32),
            pltpu.VMEM((FAR_BLOCKS, MOBA_BLOCK, wide), F32),
            pltpu.VMEM((FAR_GROUP, MOBA_BLOCK, wide), F32),
            pltpu.VMEM((FAR_GROUP, MOBA_BLOCK, wide), F32),
        ],
        compiler_params=_cparams("arbitrary", "arbitrary"),
        name="moba_prompt",
    )(tab, qt, kn, vt_pages, kmean)


PAGES_PER_STEP = 16


def _page_mean_kernel(pt_ref, *refs):
    pages, o_ref = refs[:-1], refs[-1]
    step = pl.program_id(1)
    blocks_per_step = PAGES_PER_STEP // PAGES_PER_BLOCK

    @pl.when(step == 0)
    def _():
        o_ref[...] = jnp.zeros_like(o_ref)

    col = lax.broadcasted_iota(jnp.int32, o_ref.shape, 1)
    out = o_ref[...]
    for j in range(blocks_per_step):
        tot = pages[j * PAGES_PER_BLOCK][...]
        for p in range(1, PAGES_PER_BLOCK):
            tot = tot + pages[j * PAGES_PER_BLOCK + p][...]
        mean = jnp.sum(tot, axis=-1, keepdims=True).reshape(D_MODEL, 1) * (1.0 / MOBA_BLOCK)
        out = jnp.where(col == step * blocks_per_step + j, mean, out)
    o_ref[...] = out


def page_block_means(cache_kt, page_table):
    db, n_pages = page_table.shape
    assert n_pages % PAGES_PER_STEP == 0
    n_full = n_pages // PAGES_PER_BLOCK

    def page_spec(p):
        return pl.BlockSpec((None, N_HEADS, HEAD_DIM, PAGE_SIZE),
                            lambda i, j, pt: (pt[i * n_pages + j * PAGES_PER_STEP + p], 0, 0, 0))

    return pl.pallas_call(
        _page_mean_kernel,
        grid_spec=pltpu.PrefetchScalarGridSpec(
            num_scalar_prefetch=1,
            grid=(db, n_pages // PAGES_PER_STEP),
            in_specs=[page_spec(p) for p in range(PAGES_PER_STEP)],
            out_specs=pl.BlockSpec((None, D_MODEL, n_full), lambda i, j, pt: (i, 0, 0)),
        ),
        out_shape=jax.ShapeDtypeStruct((db, D_MODEL, n_full), F32),
        compiler_params=_cparams("parallel", "arbitrary"),
        name="page_block_means",
    )(page_table.reshape(-1), *([cache_kt] * PAGES_PER_STEP))


def _gate_topk_kernel(qt_ref, kmt_ref, segt_ref, pt_ref, o_ref, *, n_full, t_new):
    kmt = kmt_ref[...]
    blk_f = lax.broadcasted_iota(jnp.int32, (N_HEADS, n_full), 1).astype(F32)
    for t in range(t_new):
        prod = kmt * qt_ref[:, t:t + 1]
        g = jnp.dot(segt_ref[...], prod, preferred_element_type=F32, precision=lax.Precision.HIGHEST)
        for j in range(TOP_K):
            top = jnp.max(g, axis=1, keepdims=True)
            first = jnp.min(jnp.where(g == top, blk_f, float(n_full)), axis=1, keepdims=True)
            pick = blk_f == first
            base = (t * TOP_K + j) * (1 + PAGES_PER_BLOCK)
            o_ref[:, base:base + 1] = first.astype(jnp.int32)
            for p in range(PAGES_PER_BLOCK):
                page = jnp.sum(jnp.where(pick, pt_ref[p:p + 1, :], 0.0), axis=1, keepdims=True)
                o_ref[:, base + 1 + p:base + 2 + p] = page.astype(jnp.int32)
            g = jnp.where(pick, NEG_INF, g)


def gate_topk(qt, kmt, page_table):
    db, _, t_new = qt.shape
    n_full = kmt.shape[2]
    width = t_new * TOP_K * (1 + PAGES_PER_BLOCK)
    segt = (jnp.arange(D_MODEL)[None, :] // HEAD_DIM == jnp.arange(N_HEADS)[:, None]).astype(F32)
    pt = page_table.reshape(db, n_full, PAGES_PER_BLOCK).transpose(0, 2, 1).astype(F32)
    out = pl.pallas_call(
        functools.partial(_gate_topk_kernel, n_full=n_full, t_new=t_new),
        grid=(db,),
        in_specs=[
            pl.BlockSpec((None, D_MODEL, t_new), lambda i: (i, 0, 0)),
            pl.BlockSpec((None, D_MODEL, n_full), lambda i: (i, 0, 0)),
            pl.BlockSpec((N_HEADS, D_MODEL), lambda i: (0, 0)),
            pl.BlockSpec((None, PAGES_PER_BLOCK, n_full), lambda i: (i, 0, 0)),
        ],
        out_specs=pl.BlockSpec((None, N_HEADS, width), lambda i: (i, 0, 0)),
        out_shape=jax.ShapeDtypeStruct((db, N_HEADS, width), jnp.int32),
        compiler_params=_cparams("parallel"),
        name="gate_topk",
    )(qt, kmt, segt, pt)
    return out.reshape(db, N_HEADS, t_new, TOP_K, 1 + PAGES_PER_BLOCK)


def _moba_sample_kernel(sel_ref, tab_ref, qt_ref, knt_ref, vnt_ref, ck_hbm, cv_hbm, o_ref, kbuf, vbuf, sems,
                        *, t_new, past, n_steps):
    n_sel_pages = t_new * TOP_K * PAGES_PER_BLOCK
    width = t_new * TOP_K * (1 + PAGES_PER_BLOCK)
    b = pl.program_id(0)
    h = pl.program_id(1)
    step = b * N_HEADS + h
    slot = step % 2

    def page_copies(st, sl):
        head = st % N_HEADS
        copies = []
        for i in range(n_sel_pages):
            tj, p = divmod(i, PAGES_PER_BLOCK)
            page = sel_ref[st * width + tj * (1 + PAGES_PER_BLOCK) + 1 + p]
            copies.append(pltpu.make_async_copy(ck_hbm.at[page, head], kbuf.at[sl, i], sems.at[0, sl]))
            copies.append(pltpu.make_async_copy(cv_hbm.at[page, head], vbuf.at[sl, i], sems.at[1, sl]))
        return copies

    @pl.when(step == 0)
    def _():
        for cp in page_copies(0, 0):
            cp.start()

    @pl.when(step + 1 < n_steps)
    def _():
        for cp in page_copies(step + 1, 1 - slot):
            cp.start()

    for cp in page_copies(step, slot):
        cp.wait()
    k_pages = [kbuf.at[slot, i] for i in range(n_sel_pages)]
    v_pages = [vbuf.at[slot, i] for i in range(n_sel_pages)]
    new_id = lax.broadcasted_iota(jnp.int32, (1, t_new), 1)
    row_id = lax.broadcasted_iota(jnp.int32, (1, PAGE_SIZE), 1)
    knt = knt_ref[...]
    vnt = vnt_ref[...]
    stride = TOP_K * (1 + PAGES_PER_BLOCK)
    for t in range(t_new):
        q = qt_ref[:, t:t + 1]
        s_own = jnp.sum(knt * q, axis=0, keepdims=True)
        s_own = s_own + _bias_from_bucket(_t5_bucket(t - new_id), tab_ref, h)
        s_own = jnp.where(new_id <= t, s_own, NEG_INF)
        scores = []
        for j in range(TOP_K):
            idx = sel_ref[((b * N_HEADS + h) * t_new + t) * stride + j * (1 + PAGES_PER_BLOCK)]
            for p in range(PAGES_PER_BLOCK):
                pg = (t * TOP_K + j) * PAGES_PER_BLOCK + p
                s = jnp.sum(k_pages[pg][...] * q, axis=0, keepdims=True)
                d0 = past + t - (idx * MOBA_BLOCK + p * PAGE_SIZE)
                scores.append(s + _bias_from_bucket(_t5_bucket(d0 - row_id), tab_ref, h))
        m = jnp.max(s_own, axis=1, keepdims=True)
        for s in scores:
            m = jnp.maximum(m, jnp.max(s, axis=1, keepdims=True))
        e_own = jnp.exp(s_own - m)
        den = jnp.sum(e_own, axis=1, keepdims=True)
        num = jnp.sum(vnt * e_own, axis=1, keepdims=True)
        weighted = jnp.zeros((HEAD_DIM, PAGE_SIZE), F32)
        for i, s in enumerate(scores):
            e = jnp.exp(s - m)
            den = den + jnp.sum(e, axis=1, keepdims=True)
            weighted = weighted + v_pages[t * TOP_K * PAGES_PER_BLOCK + i][...] * e
        num = num + jnp.sum(weighted, axis=1, keepdims=True)
        o_ref[:, t:t + 1] = num / den


def moba_sample(qt, knt, vnt, cache_kt, cache_vt, sel, tab, *, past):
    db, _, _, t_new = qt.shape
    n_sel_pages = t_new * TOP_K * PAGES_PER_BLOCK
    col_spec = pl.BlockSpec((None, None, HEAD_DIM, t_new), lambda i, h, se: (i, h, 0, 0))
    page_buf = pltpu.VMEM((2, n_sel_pages, HEAD_DIM, PAGE_SIZE), F32)
    return pl.pallas_call(
        functools.partial(_moba_sample_kernel, t_new=t_new, past=past, n_steps=db * N_HEADS),
        grid_spec=pltpu.PrefetchScalarGridSpec(
            num_scalar_prefetch=1,
            grid=(db, N_HEADS),
            in_specs=[pl.BlockSpec(memory_space=pltpu.SMEM), col_spec, col_spec, col_spec,
                      pl.BlockSpec(memory_space=pl.ANY), pl.BlockSpec(memory_space=pl.ANY)],
            out_specs=col_spec,
            scratch_shapes=[page_buf, page_buf, pltpu.SemaphoreType.DMA((2, 2))],
        ),
        out_shape=jax.ShapeDtypeStruct((db, N_HEADS, HEAD_DIM, t_new), F32),
        compiler_params=_cparams("arbitrary", "arbitrary"),
        name="moba_sample",
    )(sel, tab, qt, knt, vnt, cache_kt, cache_vt)


def _bf16_weights(w_ffn_in, w_ffn_out, w_pw1, w_pw2, w_kv, w_q, w_o):
    return tuple(w.astype(BF16) for w in (w_ffn_in, w_ffn_out, w_pw1, w_pw2, w_kv, w_q, w_o))


def kernel(x_prompt, x_sample, state_conv, cache_k, cache_v, page_table, norm_ffn, w_ffn_in, w_ffn_out,
           norm_mix, w_pw1, b_pw1, w_dw, b_dw, ln_g, ln_b, w_pw2, b_pw2, norm_kv, w_kv, k_gain, w_q, q_gain,
           w_o, rel_bias):
    bsz, seq, _ = x_prompt.shape
    dbs, t_new, _ = x_sample.shape
    n_pages = page_table.shape[1]
    past = n_pages * PAGE_SIZE
    assert past % MOBA_BLOCK == 0 and t_new <= SUBLANES and seq % MOBA_BLOCK == 0
    n_tok_s = dbs * t_new
    w_ffn_in, w_ffn_out, w_pw1, w_pw2, w_kv, w_q, w_o = _bf16_weights(
        w_ffn_in, w_ffn_out, w_pw1, w_pw2, w_kv, w_q, w_o)
    tab = rel_bias.T
    zero_bias = jnp.zeros((D_MODEL,), F32)

    xp = x_prompt.reshape(bsz * seq, D_MODEL)
    xs = x_sample.reshape(n_tok_s, D_MODEL)
    tm_p, tm_s = 512, n_tok_s

    xp = ffn(xp, norm_ffn[0, 0], w_ffn_in[0, 0], w_ffn_out[0, 0], tm=tm_p)
    xs = ffn(xs, norm_ffn[0, 0], w_ffn_in[0, 0], w_ffn_out[0, 0], tm=tm_s)

    vp = glu(xp, norm_mix[0], w_pw1[0], b_pw1[0], tm=tm_p).reshape(bsz, seq, D_MODEL)
    vs = glu(xs, norm_mix[0], w_pw1[0], b_pw1[0], tm=tm_s).reshape(dbs, t_new, D_MODEL)
    prev_p = jnp.zeros((bsz, CONV_HALO, D_MODEL), F32)
    prev_s = jnp.pad(state_conv[0], ((0, 0), (CONV_PAD, 0), (0, 0)))
    vs_pad = jnp.pad(vs, ((0, 0), (0, SUBLANES - t_new), (0, 0)))
    yp, st_p = dwconv(prev_p, vp, w_dw[0], b_dw[0], ln_g[0], ln_b[0], ts=PAGE_SIZE, valid=PAGE_SIZE)
    ys, st_s = dwconv(prev_s, vs_pad, w_dw[0], b_dw[0], ln_g[0], ln_b[0], ts=SUBLANES, valid=t_new)
    conv_prompt = st_p[None, :, CONV_PAD:]
    conv_sample = st_s[None, :, CONV_PAD:]
    xp = linear_res(yp.reshape(bsz * seq, D_MODEL), w_pw2[0], b_pw2[0], xp, tm=tm_p)
    xs = linear_res(ys[:, :t_new].reshape(n_tok_s, D_MODEL), w_pw2[0], b_pw2[0], xs, tm=tm_s)

    xp = ffn(xp, norm_ffn[0, 1], w_ffn_in[0, 1], w_ffn_out[0, 1], tm=tm_p)
    xs = ffn(xs, norm_ffn[0, 1], w_ffn_in[0, 1], w_ffn_out[0, 1], tm=tm_s)

    kt_pp, vt_pp, kn_p, kmt_p = shared_kv(xp.reshape(bsz, seq, D_MODEL), norm_kv, w_kv, k_gain)
    xs_blk = jnp.pad(xs, ((0, MOBA_BLOCK - n_tok_s), (0, 0)))[None]
    kt_sp, vt_sp, _, _ = shared_kv(xs_blk, norm_kv, w_kv, k_gain)

    def sample_cols(pages_t):
        return pages_t[0, 0, :, :, :n_tok_s].reshape(N_HEADS, HEAD_DIM, dbs, t_new).transpose(2, 0, 1, 3)

    knt_s, vnt_s = sample_cols(kt_sp), sample_cols(vt_sp)
    k_pp, v_pp = kt_pp.swapaxes(3, 4), vt_pp.swapaxes(3, 4)
    k_rows, v_rows = knt_s.swapaxes(2, 3), vnt_s.swapaxes(2, 3)
    cache_kt, cache_vt = cache_k.swapaxes(2, 3), cache_v.swapaxes(2, 3)

    xp = ffn(xp, norm_ffn[1, 0], w_ffn_in[1, 0], w_ffn_out[1, 0], tm=tm_p)
    xs = ffn(xs, norm_ffn[1, 0], w_ffn_in[1, 0], w_ffn_out[1, 0], tm=tm_s)

    qt_p = q_proj(xp.reshape(bsz, seq, D_MODEL), norm_mix[1], w_q[0], q_gain[0], tm=tm_p)
    kmean_p = kmt_p.reshape(bsz, N_HEADS, HEAD_DIM, -1).swapaxes(2, 3)
    at_p = moba_prompt(tab, qt_p, kn_p, vt_pp, kmean_p)
    xp = linear_res(at_p, w_o[0], zero_bias, xp, tm=tm_p, transposed=True)

    qt_s = q_proj(xs[None], norm_mix[1], w_q[0], q_gain[0], tm=n_tok_s)[0]
    qt_s = qt_s.reshape(D_MODEL, dbs, t_new).transpose(1, 0, 2)
    kmt_s = page_block_means(cache_kt, page_table)
    sel = gate_topk(qt_s, kmt_s, page_table)
    at_s = moba_sample(qt_s.reshape(dbs, N_HEADS, HEAD_DIM, t_new), knt_s, vnt_s, cache_kt, cache_vt,
                       sel.reshape(-1), tab, past=past)
    a_s = at_s.transpose(0, 3, 1, 2).reshape(n_tok_s, D_MODEL)
    xs = linear_res(a_s, w_o[0], zero_bias, xs, tm=tm_s)

    xp = ffn(xp, norm_ffn[1, 1], w_ffn_in[1, 1], w_ffn_out[1, 1], tm=tm_p)
    xs = ffn(xs, norm_ffn[1, 1], w_ffn_in[1, 1], w_ffn_out[1, 1], tm=tm_s)

    return (xp.reshape(bsz, seq, D_MODEL), xs.reshape(dbs, t_new, D_MODEL), conv_prompt, conv_sample,
            k_pp, v_pp, k_rows, v_rows)
```

```python
import functools
import math

import numpy as np
import jax
import jax.numpy as jnp
from jax import lax
from jax.experimental import pallas as pl
from jax.experimental.pallas import tpu as pltpu

D_MODEL = 1024
N_HEADS = 16
HEAD_DIM = D_MODEL // N_HEADS
D_FF = 2816
CONV_W = 31
MOBA_BLOCK = 256
TOP_K = 3
PAGE_SIZE = 128
PAGES_PER_BLOCK = MOBA_BLOCK // PAGE_SIZE
N_BUCKETS = 32
MAX_EXACT = N_BUCKETS // 2
REL_MAX_DIST = 2048
EPS = 1e-6
SCALE = HEAD_DIM ** -0.5

LANES = 128
SUBLANES = 8
VMEM_LIMIT = 56 * 1024 * 1024

F32 = jnp.float32
BF16 = jnp.bfloat16
NEG_INF = float("-inf")

CONV_HALO = 32
CONV_PAD = CONV_HALO - (CONV_W - 1)


def _first_far_distance():
    n = np.arange(1, 4 * REL_MAX_DIST, dtype=np.float32)
    large = MAX_EXACT + (np.log(n / np.float32(MAX_EXACT)) / np.float32(math.log(REL_MAX_DIST / MAX_EXACT))
                         * np.float32(N_BUCKETS - MAX_EXACT)).astype(np.int32)
    bucket = np.where(n < MAX_EXACT, n.astype(np.int32), np.minimum(large, N_BUCKETS - 1))
    return int(n[np.argmax(bucket == N_BUCKETS - 1)])


FAR_DIST = _first_far_distance()
FAR_BLOCKS = -(-(FAR_DIST + MOBA_BLOCK - 1) // MOBA_BLOCK)


def _cparams(*sem):
    return pltpu.CompilerParams(dimension_semantics=sem, vmem_limit_bytes=VMEM_LIMIT)


def _rms(x, g):
    ms = jnp.mean(x * x, axis=-1, keepdims=True)
    return x * lax.rsqrt(ms + EPS) * g


def _t5_bucket(delta):
    n = jnp.maximum(delta, 0)
    nf = jnp.maximum(n, 1).astype(F32)
    large = MAX_EXACT + (jnp.log(nf / MAX_EXACT) / math.log(REL_MAX_DIST / MAX_EXACT)
                         * (N_BUCKETS - MAX_EXACT)).astype(jnp.int32)
    large = jnp.minimum(large, N_BUCKETS - 1)
    return jnp.where(n < MAX_EXACT, n, large)


def _bias_from_bucket(bucket, tab_ref, h):
    val = jnp.zeros(bucket.shape, F32)
    for bb in range(N_BUCKETS):
        val = jnp.where(bucket == bb, tab_ref[h, bb], val)
    return val


MAX_PAGES_PER_STEP = 16


def _page_block_means(pages, o_ref, step, steps_per_seq):
    blocks_per_step = len(pages) // PAGES_PER_BLOCK
    sub = step % steps_per_seq

    @pl.when(sub == 0)
    def _():
        o_ref[...] = jnp.zeros_like(o_ref)

    col = lax.broadcasted_iota(jnp.int32, o_ref.shape, 1)
    out = o_ref[...]
    for j in range(blocks_per_step):
        tot = pages[j * PAGES_PER_BLOCK][...]
        for p in range(1, PAGES_PER_BLOCK):
            tot = tot + pages[j * PAGES_PER_BLOCK + p][...]
        mean = jnp.sum(tot, axis=-1, keepdims=True).reshape(D_MODEL, 1) * (1.0 / MOBA_BLOCK)
        out = jnp.where(col == sub * blocks_per_step + j, mean, out)
    o_ref[...] = out


def _ffn_kernel(*refs, tf, n_pages, steps_per_seq):
    if n_pages:
        refs = refs[1:]
    x_ref, g_ref, win_ref, wout_ref = refs[:4]
    pages = refs[4:4 + n_pages]
    o_ref = refs[4 + n_pages]
    h_ref = refs[-1]
    if n_pages:
        _page_block_means(pages, refs[5 + n_pages], pl.program_id(0), steps_per_seq)
    x = x_ref[...]
    h_ref[...] = _rms(x, g_ref[...]).astype(BF16)
    acc = None
    for j in range(D_FF // tf):
        h = h_ref[...]
        a = jnp.dot(h, win_ref[:, j * tf:(j + 1) * tf], preferred_element_type=F32)
        g = jnp.dot(h, win_ref[:, D_FF + j * tf:D_FF + (j + 1) * tf], preferred_element_type=F32)
        act = (a * jax.nn.sigmoid(a) * g).astype(BF16)
        part = jnp.dot(act, wout_ref[j * tf:(j + 1) * tf, :], preferred_element_type=F32)
        acc = part if acc is None else acc + part
    o_ref[...] = x + 0.5 * acc


def ffn(x, g, w_in, w_out, layer, half, *, tm, tf=D_FF // 2, pages=None):
    t = x.shape[0]
    steps = t // tm
    wmap = lambda i, *_: (layer, half, 0, 0)
    in_specs = [
        pl.BlockSpec((tm, D_MODEL), lambda i, *_: (i, 0)),
        pl.BlockSpec((1, D_MODEL), lambda i, *_: (0, 0)),
        pl.BlockSpec((None, None, D_MODEL, 2 * D_FF), wmap, pipeline_mode=pl.Buffered(1)),
        pl.BlockSpec((None, None, D_FF, D_MODEL), wmap, pipeline_mode=pl.Buffered(1)),
    ]
    x_spec = pl.BlockSpec((tm, D_MODEL), lambda i, *_: (i, 0))
    x_shape = jax.ShapeDtypeStruct((t, D_MODEL), F32)
    scratch = [pltpu.VMEM((tm, D_MODEL), BF16)]
    args = (x, g.reshape(1, D_MODEL), w_in, w_out)
    if pages is None:
        return pl.pallas_call(
            functools.partial(_ffn_kernel, tf=tf, n_pages=0, steps_per_seq=0),
            grid=(steps,), in_specs=in_specs, out_specs=x_spec, out_shape=x_shape, scratch_shapes=scratch,
            compiler_params=_cparams("parallel"), name="ffn",
        )(*args)
    cache_kt, page_table, first, n_pages_seq, per_step = pages
    n_full = n_pages_seq // PAGES_PER_BLOCK
    steps_per_seq = n_pages_seq // per_step

    def page_spec(p):
        return pl.BlockSpec((None, N_HEADS, HEAD_DIM, PAGE_SIZE),
                            lambda i, pt: (pt[first + i * per_step + p], 0, 0, 0))

    return pl.pallas_call(
        functools.partial(_ffn_kernel, tf=tf, n_pages=per_step, steps_per_seq=steps_per_seq),
        grid_spec=pltpu.PrefetchScalarGridSpec(
            num_scalar_prefetch=1,
            grid=(steps,),
            in_specs=in_specs + [page_spec(p) for p in range(per_step)],
            out_specs=[x_spec, pl.BlockSpec((None, D_MODEL, n_full), lambda i, pt: (i // steps_per_seq, 0, 0))],
            scratch_shapes=scratch,
        ),
        out_shape=[x_shape, jax.ShapeDtypeStruct((steps // steps_per_seq, D_MODEL, n_full), F32)],
        compiler_params=_cparams("arbitrary"),
        name="ffn_pages",
    )(page_table, *args, *([cache_kt] * per_step))


def pages_per_ffn_step(n_ffn, steps, n_seq, n_pages_seq):
    per_step, rem = divmod(n_seq * n_pages_seq, n_ffn * steps)
    ok = (rem == 0 and per_step > 0 and per_step % PAGES_PER_BLOCK == 0 and n_pages_seq % per_step == 0
          and steps % (n_pages_seq // per_step) == 0 and per_step <= MAX_PAGES_PER_STEP)
    return per_step if ok else 0


def _glu_kernel(x_ref, g_ref, w_ref, b_ref, o_ref):
    h = _rms(x_ref[...], g_ref[...]).astype(BF16)
    ag = jnp.dot(h, w_ref[...], preferred_element_type=F32) + b_ref[...]
    a = ag[:, :D_MODEL]
    gate = ag[:, D_MODEL:]
    o_ref[...] = a * jax.nn.sigmoid(gate)


def glu(x, g, w, b, *, tm):
    t = x.shape[0]
    return pl.pallas_call(
        _glu_kernel,
        grid=(t // tm,),
        in_specs=[
            pl.BlockSpec((tm, D_MODEL), lambda i: (i, 0)),
            pl.BlockSpec((1, D_MODEL), lambda i: (0, 0)),
            pl.BlockSpec((D_MODEL, 2 * D_MODEL), lambda i: (0, 0)),
            pl.BlockSpec((1, 2 * D_MODEL), lambda i: (0, 0)),
        ],
        out_specs=pl.BlockSpec((tm, D_MODEL), lambda i: (i, 0)),
        out_shape=jax.ShapeDtypeStruct((t, D_MODEL), F32),
        compiler_params=_cparams("parallel"),
        name="glu",
    )(x, g.reshape(1, D_MODEL), w, b.reshape(1, 2 * D_MODEL))


def _dwconv_kernel(prev_ref, v_ref, wdw_ref, bdw_ref, lng_ref, lnb_ref, y_ref, state_ref, buf_ref,
                   shift_ref, *, ts, valid):
    @pl.when(pl.program_id(1) == 0)
    def _():
        buf_ref[0:CONV_HALO, :] = prev_ref[...]

    buf_ref[CONV_HALO:CONV_HALO + ts, :] = v_ref[...]
    for c in range(D_MODEL // LANES):
        lanes = slice(c * LANES, (c + 1) * LANES)
        acc = jnp.zeros((ts, LANES), F32)
        for r in range(SUBLANES):
            taps = range(r, CONV_W, SUBLANES)
            span = taps[-1] - r + ts
            shift_ref[r % 2, 0:span, :] = buf_ref[CONV_PAD + r:CONV_PAD + r + span, lanes]
            for w in taps:
                acc = acc + shift_ref[r % 2, w - r:w - r + ts, :] * wdw_ref[w:w + 1, lanes]
        y_ref[:, lanes] = acc + bdw_ref[:, lanes]
    y = y_ref[...]
    mu = jnp.mean(y, axis=-1, keepdims=True)
    var = jnp.mean(jnp.square(y - mu), axis=-1, keepdims=True)
    yn = (y - mu) * lax.rsqrt(var + EPS) * lng_ref[...] + lnb_ref[...]
    y_ref[...] = yn * jax.nn.sigmoid(yn)
    state_ref[...] = buf_ref[valid:valid + CONV_HALO, :]
    buf_ref[0:CONV_HALO, :] = buf_ref[ts:ts + CONV_HALO, :]


def dwconv(prev, v, w_dw, b_dw, ln_g, ln_b, *, ts, valid):
    b, s, _ = v.shape
    row = lambda a: a.reshape(1, D_MODEL)
    return pl.pallas_call(
        functools.partial(_dwconv_kernel, ts=ts, valid=valid),
        grid=(b, s // ts),
        in_specs=[
            pl.BlockSpec((None, CONV_HALO, D_MODEL), lambda i, j: (i, 0, 0)),
            pl.BlockSpec((None, ts, D_MODEL), lambda i, j: (i, j, 0)),
            pl.BlockSpec((CONV_W, D_MODEL), lambda i, j: (0, 0)),
            pl.BlockSpec((1, D_MODEL), lambda i, j: (0, 0)),
            pl.BlockSpec((1, D_MODEL), lambda i, j: (0, 0)),
            pl.BlockSpec((1, D_MODEL), lambda i, j: (0, 0)),
        ],
        out_specs=[
            pl.BlockSpec((None, ts, D_MODEL), lambda i, j: (i, j, 0)),
            pl.BlockSpec((None, CONV_HALO, D_MODEL), lambda i, j: (i, 0, 0)),
        ],
        out_shape=[
            jax.ShapeDtypeStruct((b, s, D_MODEL), F32),
            jax.ShapeDtypeStruct((b, CONV_HALO, D_MODEL), F32),
        ],
        scratch_shapes=[pltpu.VMEM((CONV_HALO + ts, D_MODEL), F32),
                        pltpu.VMEM((2, CONV_HALO + ts, LANES), F32)],
        compiler_params=_cparams("parallel", "arbitrary"),
        name="dwconv",
    )(prev, v, w_dw, row(b_dw), row(ln_g), row(ln_b))


def _linear_res_kernel(y_ref, w_ref, b_ref, x_ref, o_ref, *, transposed):
    y = y_ref[...]
    if transposed:
        y = y.T
    o_ref[...] = x_ref[...] + jnp.dot(y.astype(BF16), w_ref[...], preferred_element_type=F32) + b_ref[...]


def linear_res(y, w, b, x, *, tm, transposed=False):
    t = x.shape[0]
    if transposed:
        tiles = y.shape[2] // tm
        y_spec = pl.BlockSpec((None, D_MODEL, tm), lambda i: (i // tiles, 0, i % tiles))
    else:
        y_spec = pl.BlockSpec((tm, D_MODEL), lambda i: (i, 0))
    return pl.pallas_call(
        functools.partial(_linear_res_kernel, transposed=transposed),
        grid=(t // tm,),
        in_specs=[
            y_spec,
            pl.BlockSpec((D_MODEL, D_MODEL), lambda i: (0, 0)),
            pl.BlockSpec((1, D_MODEL), lambda i: (0, 0)),
            pl.BlockSpec((tm, D_MODEL), lambda i: (i, 0)),
        ],
        out_specs=pl.BlockSpec((tm, D_MODEL), lambda i: (i, 0)),
        out_shape=jax.ShapeDtypeStruct((t, D_MODEL), F32),
        compiler_params=_cparams("parallel"),
        name="linear_res",
    )(y, w, b.reshape(1, D_MODEL), x)


def _kv_kernel(x_ref, g_ref, w_ref, kgain_ref, kt_ref, vt_ref, kn_ref, kmt_ref, knt_ref):
    j = pl.program_id(1)
    h = _rms(x_ref[...], g_ref[...]).astype(BF16)
    kv = jnp.dot(h, w_ref[...], preferred_element_type=F32)
    kt = kv[:, :D_MODEL].T
    vt = kv[:, D_MODEL:].T
    for hd in range(N_HEADS):
        rows = slice(hd * HEAD_DIM, (hd + 1) * HEAD_DIM)
        kh = kt[rows, :]
        ms = jnp.mean(kh * kh, axis=0, keepdims=True)
        khn = kh * lax.rsqrt(ms + EPS) * kgain_ref[...]
        knt_ref[rows, :] = khn
        for p in range(PAGES_PER_BLOCK):
            toks = slice(p * PAGE_SIZE, (p + 1) * PAGE_SIZE)
            kt_ref[p, hd] = khn[:, toks]
            vt_ref[p, hd] = vt[rows, toks]
    knt = knt_ref[...]
    kn_ref[...] = knt.T.astype(BF16)
    kmean = jnp.sum(knt, axis=1, keepdims=True) * (1.0 / MOBA_BLOCK)

    @pl.when(j == 0)
    def _():
        kmt_ref[...] = jnp.zeros_like(kmt_ref)

    col = lax.broadcasted_iota(jnp.int32, kmt_ref.shape, 1)
    kmt_ref[...] = jnp.where(col == j, kmean, kmt_ref[...])


def shared_kv(x, g, w, k_gain):
    b, s, _ = x.shape
    n_blk = s // MOBA_BLOCK
    pages = (b, s // PAGE_SIZE, N_HEADS, HEAD_DIM, PAGE_SIZE)
    page_spec = pl.BlockSpec((None, PAGES_PER_BLOCK, N_HEADS, HEAD_DIM, PAGE_SIZE),
                             lambda i, j: (i, j, 0, 0, 0))
    return pl.pallas_call(
        _kv_kernel,
        grid=(b, n_blk),
        in_specs=[
            pl.BlockSpec((None, MOBA_BLOCK, D_MODEL), lambda i, j: (i, j, 0)),
            pl.BlockSpec((1, D_MODEL), lambda i, j: (0, 0)),
            pl.BlockSpec((D_MODEL, 2 * D_MODEL), lambda i, j: (0, 0)),
            pl.BlockSpec((HEAD_DIM, 1), lambda i, j: (0, 0)),
        ],
        out_specs=[
            page_spec,
            page_spec,
            pl.BlockSpec((None, MOBA_BLOCK, D_MODEL), lambda i, j: (i, j, 0)),
            pl.BlockSpec((None, D_MODEL, n_blk), lambda i, j: (i, 0, 0)),
        ],
        out_shape=[
            jax.ShapeDtypeStruct(pages, F32),
            jax.ShapeDtypeStruct(pages, F32),
            jax.ShapeDtypeStruct((b, s, D_MODEL), BF16),
            jax.ShapeDtypeStruct((b, D_MODEL, n_blk), F32),
        ],
        scratch_shapes=[pltpu.VMEM((D_MODEL, MOBA_BLOCK), F32)],
        compiler_params=_cparams("parallel", "arbitrary"),
        name="shared_kv",
    )(x, g.reshape(1, D_MODEL), w, k_gain.reshape(HEAD_DIM, 1))


def _q_kernel(x_ref, g_ref, w_ref, qgain_ref, qt_ref):
    h = _rms(x_ref[...], g_ref[...]).astype(BF16)
    qt = jnp.dot(h, w_ref[...], preferred_element_type=F32).T
    for hd in range(N_HEADS):
        rows = slice(hd * HEAD_DIM, (hd + 1) * HEAD_DIM)
        qh = qt[rows, :]
        ms = jnp.mean(qh * qh, axis=0, keepdims=True)
        qt_ref[rows, :] = qh * lax.rsqrt(ms + EPS) * (qgain_ref[...] * SCALE)


def q_proj(x, g, w, q_gain, *, tm):
    b, s, _ = x.shape
    return pl.pallas_call(
        _q_kernel,
        grid=(b, s // tm),
        in_specs=[
            pl.BlockSpec((None, tm, D_MODEL), lambda i, j: (i, j, 0)),
            pl.BlockSpec((1, D_MODEL), lambda i, j: (0, 0)),
            pl.BlockSpec((D_MODEL, D_MODEL), lambda i, j: (0, 0)),
            pl.BlockSpec((HEAD_DIM, 1), lambda i, j: (0, 0)),
        ],
        out_specs=pl.BlockSpec((None, D_MODEL, tm), lambda i, j: (i, 0, j)),
        out_shape=jax.ShapeDtypeStruct((b, D_MODEL, s), F32),
        compiler_params=_cparams("parallel", "parallel"),
        name="q_proj",
    )(x, g.reshape(1, D_MODEL), w, q_gain.reshape(HEAD_DIM, 1))


FAR_GROUP = 4
NEAR_SPLIT = 4
HEAD_PAIR = 2
LOG2E = math.log2(math.e)
ONES_ROWS = 16


def _moba_prompt_kernel(tab_ref, qt_ref, kn_ref, vt_ref, kmean_ref, o_ref, bias_ref, sel_ref, near_ref,
                        far0_ref, far1_ref, *, n_blk):
    hp = pl.program_id(0)
    blk = MOBA_BLOCK
    wide = HEAD_PAIR * blk
    kk = lax.broadcasted_iota(jnp.int32, (blk, blk), 0)
    qq = lax.broadcasted_iota(jnp.int32, (blk, blk), 1)

    @pl.when(pl.program_id(1) == 0)
    def _():
        for d in range(FAR_BLOCKS):
            bucket = _t5_bucket(d * blk + qq - kk)
            for e in range(HEAD_PAIR):
                bias_ref[d, :, e * blk:(e + 1) * blk] = (
                    _bias_from_bucket(bucket, tab_ref, HEAD_PAIR * hp + e) * LOG2E)

    far_bias = jnp.concatenate(
        [jnp.full((1, blk), tab_ref[HEAD_PAIR * hp + e, N_BUCKETS - 1] * LOG2E, F32) for e in range(HEAD_PAIR)],
        axis=1)
    causal = jnp.concatenate([kk <= qq] * HEAD_PAIR, axis=1)
    blk_f = lax.broadcasted_iota(jnp.int32, (n_blk, wide), 0).astype(F32)
    zeros_q = jnp.zeros((HEAD_DIM, blk), F32)

    def q_block(c, carry):
        q0 = pl.multiple_of(c * blk, blk)
        qt = qt_ref[:, pl.ds(q0, blk)]
        qs = [qt[e * HEAD_DIM:(e + 1) * HEAD_DIM] for e in range(HEAD_PAIR)]
        q2 = jnp.concatenate(
            [jnp.concatenate([qs[0] * LOG2E, zeros_q], axis=0), jnp.concatenate([zeros_q, qs[1] * LOG2E], axis=0)],
            axis=1).astype(BF16)
        gate = jnp.concatenate(
            [jnp.dot(kmean_ref[e], qs[e], preferred_element_type=F32, precision=lax.Precision.HIGHEST)
             for e in range(HEAD_PAIR)], axis=1)
        past = blk_f < lax.convert_element_type(c, F32)
        g = jnp.where(past, gate, NEG_INF)
        addmask = jnp.full((n_blk, wide), NEG_INF, F32)
        for _ in range(TOP_K):
            top = jnp.max(g, axis=0, keepdims=True)
            first = jnp.min(jnp.where(g == top, blk_f, float(n_blk)), axis=0, keepdims=True)
            pick = blk_f == first
            addmask = jnp.where(pick, jnp.where(past, 0.0, NEG_INF), addmask)
            g = jnp.where(pick, NEG_INF, g)
        sel_ref[...] = addmask

        def scores(kb):
            keys = kn_ref[pl.ds(pl.multiple_of(kb * blk, blk), blk), :]
            return jnp.dot(keys, q2, preferred_element_type=F32)

        def values(kb, e):
            pages = [vt_ref[kb * PAGES_PER_BLOCK + p, e] for p in range(PAGES_PER_BLOCK)]
            return jnp.concatenate(pages, axis=1).astype(BF16)

        def attend(ma, s_ref, rows, kbs, first=0):
            m, acc = ma
            tops = [jnp.max(s_ref[first + i], axis=0, keepdims=True) for i in range(len(kbs))]
            tops = [t if row is None else t + row for t, row in zip(tops, rows)]
            m_new = functools.reduce(jnp.maximum, tops, m)
            ps = jnp.concatenate(
                [jnp.exp2(s_ref[first + i] - (m_new if row is None else m_new - row)).astype(BF16)
                 for i, row in enumerate(rows)], axis=0)
            ones = jnp.ones((ONES_ROWS, ps.shape[0]), BF16)
            pvs = []
            for e in range(HEAD_PAIR):
                vals = jnp.concatenate([values(kb, e) for kb in kbs], axis=1)
                pvs.append(jnp.dot(jnp.concatenate([vals, ones], axis=0), ps[:, e * blk:(e + 1) * blk],
                                   preferred_element_type=F32))
            return m_new, jnp.exp2(m - m_new) * acc + jnp.concatenate(pvs, axis=1)

        n_far = jnp.maximum(c - (FAR_BLOCKS - 1), 0)

        def far_scores(g, s_ref):
            for j in range(FAR_GROUP):
                s_ref[j] = scores(jnp.minimum(g * FAR_GROUP + j, n_blk - 1))

        def far_attend(g, s_ref, ma):
            kbs = [g * FAR_GROUP + j for j in range(FAR_GROUP)]
            kbc = [jnp.minimum(kb, n_blk - 1) for kb in kbs]
            rows = [jnp.where(kb < n_far, sel_ref[pl.ds(k, 1), :] + far_bias, NEG_INF)
                    for kb, k in zip(kbs, kbc)]
            return attend(ma, s_ref, rows, kbc)

        far_scores(0, far0_ref)
        near_ref[0] = jnp.where(causal, scores(c) + bias_ref[0], NEG_INF)
        rows, kbs = [None], [c]
        for d in range(1, FAR_BLOCKS):
            kb = jnp.maximum(c - d, 0)
            near_ref[d] = scores(kb) + bias_ref[d]
            rows.append(jnp.where(c - d >= 0, sel_ref[pl.ds(kb, 1), :], NEG_INF))
            kbs.append(kb)
        init = (jnp.full((1, wide), NEG_INF, F32), jnp.zeros((HEAD_DIM + ONES_ROWS, wide), F32))
        ma = attend(init, near_ref, rows[:NEAR_SPLIT], kbs[:NEAR_SPLIT])
        ma = attend(ma, near_ref, rows[NEAR_SPLIT:], kbs[NEAR_SPLIT:], first=NEAR_SPLIT)

        def far_pair(i, ma):
            far_scores(2 * i + 1, far1_ref)
            ma = far_attend(2 * i, far0_ref, ma)
            far_scores(2 * i + 2, far0_ref)
            return far_attend(2 * i + 1, far1_ref, ma)

        n_groups = (n_far + FAR_GROUP - 1) // FAR_GROUP
        _, acc = lax.fori_loop(0, (n_groups + 1) // 2, far_pair, ma)
        for e in range(HEAD_PAIR):
            cols = slice(e * blk, (e + 1) * blk)
            o_ref[e * HEAD_DIM:(e + 1) * HEAD_DIM, pl.ds(q0, blk)] = (
                acc[:HEAD_DIM, cols] / acc[HEAD_DIM:HEAD_DIM + 1, cols])
        return carry

    lax.fori_loop(0, n_blk, q_block, 0)


def moba_prompt(tab, qt, kn, vt_pages, kmean):
    b, _, s = qt.shape
    n_blk = s // MOBA_BLOCK
    wide = HEAD_PAIR * MOBA_BLOCK
    pair_t = pl.BlockSpec((None, HEAD_PAIR * HEAD_DIM, s), lambda hp, i: (i, hp, 0))
    return pl.pallas_call(
        functools.partial(_moba_prompt_kernel, n_blk=n_blk),
        grid=(N_HEADS // HEAD_PAIR, b),
        in_specs=[
            pl.BlockSpec(memory_space=pltpu.SMEM),
            pair_t,
            pl.BlockSpec((None, s, HEAD_PAIR * HEAD_DIM), lambda hp, i: (i, 0, hp)),
            pl.BlockSpec((None, s // PAGE_SIZE, HEAD_PAIR, HEAD_DIM, PAGE_SIZE), lambda hp, i: (i, 0, hp, 0, 0)),
            pl.BlockSpec((None, HEAD_PAIR, n_blk, HEAD_DIM), lambda hp, i: (i, hp, 0, 0)),
        ],
        out_specs=pair_t,
        out_shape=jax.ShapeDtypeStruct((b, D_MODEL, s), F32),
        scratch_shapes=[
            pltpu.VMEM((FAR_BLOCKS, MOBA_BLOCK, wide), F32),
            pltpu.VMEM((n_blk, wide), F32),
            pltpu.VMEM((FAR_BLOCKS, MOBA_BLOCK, wide), F32),
            pltpu.VMEM((FAR_GROUP, MOBA_BLOCK, wide), F32),
            pltpu.VMEM((FAR_GROUP, MOBA_BLOCK, wide), F32),
        ],
        compiler_params=_cparams("arbitrary", "arbitrary"),
        name="moba_prompt",
    )(tab, qt, kn, vt_pages, kmean)


PAGES_PER_STEP = 16


def _page_mean_kernel(pt_ref, *refs):
    _page_block_means(refs[:-1], refs[-1], pl.program_id(1), pl.num_programs(1))


def page_block_means(cache_kt, page_table):
    db, n_pages = page_table.shape
    assert n_pages % PAGES_PER_STEP == 0
    n_full = n_pages // PAGES_PER_BLOCK

    def page_spec(p):
        return pl.BlockSpec((None, N_HEADS, HEAD_DIM, PAGE_SIZE),
                            lambda i, j, pt: (pt[i * n_pages + j * PAGES_PER_STEP + p], 0, 0, 0))

    return pl.pallas_call(
        _page_mean_kernel,
        grid_spec=pltpu.PrefetchScalarGridSpec(
            num_scalar_prefetch=1,
            grid=(db, n_pages // PAGES_PER_STEP),
            in_specs=[page_spec(p) for p in range(PAGES_PER_STEP)],
            out_specs=pl.BlockSpec((None, D_MODEL, n_full), lambda i, j, pt: (i, 0, 0)),
        ),
        out_shape=jax.ShapeDtypeStruct((db, D_MODEL, n_full), F32),
        compiler_params=_cparams("parallel", "arbitrary"),
        name="page_block_means",
    )(page_table.reshape(-1), *([cache_kt] * PAGES_PER_STEP))


def _gate_topk_kernel(qt_ref, kmt_ref, segt_ref, pt_ref, o_ref, *, n_full, t_new):
    kmt = kmt_ref[...]
    blk_f = lax.broadcasted_iota(jnp.int32, (N_HEADS, n_full), 1).astype(F32)
    for t in range(t_new):
        prod = kmt * qt_ref[:, t:t + 1]
        g = jnp.dot(segt_ref[...], prod, preferred_element_type=F32, precision=lax.Precision.HIGHEST)
        for j in range(TOP_K):
            top = jnp.max(g, axis=1, keepdims=True)
            first = jnp.min(jnp.where(g == top, blk_f, float(n_full)), axis=1, keepdims=True)
            pick = blk_f == first
            base = (t * TOP_K + j) * (1 + PAGES_PER_BLOCK)
            o_ref[:, base:base + 1] = first.astype(jnp.int32)
            for p in range(PAGES_PER_BLOCK):
                page = jnp.sum(jnp.where(pick, pt_ref[p:p + 1, :], 0.0), axis=1, keepdims=True)
                o_ref[:, base + 1 + p:base + 2 + p] = page.astype(jnp.int32)
            g = jnp.where(pick, NEG_INF, g)


def gate_topk(qt, kmt, page_table):
    db, _, t_new = qt.shape
    n_full = kmt.shape[2]
    width = t_new * TOP_K * (1 + PAGES_PER_BLOCK)
    segt = (jnp.arange(D_MODEL)[None, :] // HEAD_DIM == jnp.arange(N_HEADS)[:, None]).astype(F32)
    pt = page_table.reshape(db, n_full, PAGES_PER_BLOCK).transpose(0, 2, 1).astype(F32)
    out = pl.pallas_call(
        functools.partial(_gate_topk_kernel, n_full=n_full, t_new=t_new),
        grid=(db,),
        in_specs=[
            pl.BlockSpec((None, D_MODEL, t_new), lambda i: (i, 0, 0)),
            pl.BlockSpec((None, D_MODEL, n_full), lambda i: (i, 0, 0)),
            pl.BlockSpec((N_HEADS, D_MODEL), lambda i: (0, 0)),
            pl.BlockSpec((None, PAGES_PER_BLOCK, n_full), lambda i: (i, 0, 0)),
        ],
        out_specs=pl.BlockSpec((None, N_HEADS, width), lambda i: (i, 0, 0)),
        out_shape=jax.ShapeDtypeStruct((db, N_HEADS, width), jnp.int32),
        compiler_params=_cparams("parallel"),
        name="gate_topk",
    )(qt, kmt, segt, pt)
    return out.reshape(db, N_HEADS, t_new, TOP_K, 1 + PAGES_PER_BLOCK)


def _moba_sample_kernel(sel_ref, tab_ref, qt_ref, knt_ref, vnt_ref, ck_hbm, cv_hbm, o_ref, kbuf, vbuf, sems,
                        *, t_new, past, n_steps):
    n_sel_pages = t_new * TOP_K * PAGES_PER_BLOCK
    width = t_new * TOP_K * (1 + PAGES_PER_BLOCK)
    b = pl.program_id(0)
    h = pl.program_id(1)
    step = b * N_HEADS + h
    slot = step % 2

    def page_copies(st, sl):
        head = st % N_HEADS
        copies = []
        for i in range(n_sel_pages):
            tj, p = divmod(i, PAGES_PER_BLOCK)
            page = sel_ref[st * width + tj * (1 + PAGES_PER_BLOCK) + 1 + p]
            copies.append(pltpu.make_async_copy(ck_hbm.at[page, head], kbuf.at[sl, i], sems.at[0, sl]))
            copies.append(pltpu.make_async_copy(cv_hbm.at[page, head], vbuf.at[sl, i], sems.at[1, sl]))
        return copies

    @pl.when(step == 0)
    def _():
        for cp in page_copies(0, 0):
            cp.start()

    @pl.when(step + 1 < n_steps)
    def _():
        for cp in page_copies(step + 1, 1 - slot):
            cp.start()

    for cp in page_copies(step, slot):
        cp.wait()
    k_pages = [kbuf.at[slot, i] for i in range(n_sel_pages)]
    v_pages = [vbuf.at[slot, i] for i in range(n_sel_pages)]
    new_id = lax.broadcasted_iota(jnp.int32, (1, t_new), 1)
    row_id = lax.broadcasted_iota(jnp.int32, (1, PAGE_SIZE), 1)
    knt = knt_ref[...]
    vnt = vnt_ref[...]
    stride = TOP_K * (1 + PAGES_PER_BLOCK)
    for t in range(t_new):
        q = qt_ref[:, t:t + 1]
        s_own = jnp.sum(knt * q, axis=0, keepdims=True)
        s_own = s_own + _bias_from_bucket(_t5_bucket(t - new_id), tab_ref, h)
        s_own = jnp.where(new_id <= t, s_own, NEG_INF)
        scores = []
        for j in range(TOP_K):
            idx = sel_ref[((b * N_HEADS + h) * t_new + t) * stride + j * (1 + PAGES_PER_BLOCK)]
            for p in range(PAGES_PER_BLOCK):
                pg = (t * TOP_K + j) * PAGES_PER_BLOCK + p
                s = jnp.sum(k_pages[pg][...] * q, axis=0, keepdims=True)
                d0 = past + t - (idx * MOBA_BLOCK + p * PAGE_SIZE)
                scores.append(s + _bias_from_bucket(_t5_bucket(d0 - row_id), tab_ref, h))
        m = jnp.max(s_own, axis=1, keepdims=True)
        for s in scores:
            m = jnp.maximum(m, jnp.max(s, axis=1, keepdims=True))
        e_own = jnp.exp(s_own - m)
        den = jnp.sum(e_own, axis=1, keepdims=True)
        num = jnp.sum(vnt * e_own, axis=1, keepdims=True)
        weighted = jnp.zeros((HEAD_DIM, PAGE_SIZE), F32)
        for i, s in enumerate(scores):
            e = jnp.exp(s - m)
            den = den + jnp.sum(e, axis=1, keepdims=True)
            weighted = weighted + v_pages[t * TOP_K * PAGES_PER_BLOCK + i][...] * e
        num = num + jnp.sum(weighted, axis=1, keepdims=True)
        o_ref[:, t:t + 1] = num / den


def moba_sample(qt, knt, vnt, cache_kt, cache_vt, sel, tab, *, past):
    db, _, _, t_new = qt.shape
    n_sel_pages = t_new * TOP_K * PAGES_PER_BLOCK
    col_spec = pl.BlockSpec((None, None, HEAD_DIM, t_new), lambda i, h, se: (i, h, 0, 0))
    page_buf = pltpu.VMEM((2, n_sel_pages, HEAD_DIM, PAGE_SIZE), F32)
    return pl.pallas_call(
        functools.partial(_moba_sample_kernel, t_new=t_new, past=past, n_steps=db * N_HEADS),
        grid_spec=pltpu.PrefetchScalarGridSpec(
            num_scalar_prefetch=1,
            grid=(db, N_HEADS),
            in_specs=[pl.BlockSpec(memory_space=pltpu.SMEM), col_spec, col_spec, col_spec,
                      pl.BlockSpec(memory_space=pl.ANY), pl.BlockSpec(memory_space=pl.ANY)],
            out_specs=col_spec,
            scratch_shapes=[page_buf, page_buf, pltpu.SemaphoreType.DMA((2, 2))],
        ),
        out_shape=jax.ShapeDtypeStruct((db, N_HEADS, HEAD_DIM, t_new), F32),
        compiler_params=_cparams("arbitrary", "arbitrary"),
        name="moba_sample",
    )(sel, tab, qt, knt, vnt, cache_kt, cache_vt)


def _bf16_weights(w_ffn_in, w_ffn_out, w_pw1, w_pw2, w_kv, w_q, w_o):
    return tuple(w.astype(BF16) for w in (w_ffn_in, w_ffn_out, w_pw1, w_pw2, w_kv, w_q, w_o))


def kernel(x_prompt, x_sample, state_conv, cache_k, cache_v, page_table, norm_ffn, w_ffn_in, w_ffn_out,
           norm_mix, w_pw1, b_pw1, w_dw, b_dw, ln_g, ln_b, w_pw2, b_pw2, norm_kv, w_kv, k_gain, w_q, q_gain,
           w_o, rel_bias):
    bsz, seq, _ = x_prompt.shape
    dbs, t_new, _ = x_sample.shape
    n_pages = page_table.shape[1]
    past = n_pages * PAGE_SIZE
    assert past % MOBA_BLOCK == 0 and t_new <= SUBLANES and seq % MOBA_BLOCK == 0
    n_tok_s = dbs * t_new
    w_ffn_in, w_ffn_out, w_pw1, w_pw2, w_kv, w_q, w_o = _bf16_weights(
        w_ffn_in, w_ffn_out, w_pw1, w_pw2, w_kv, w_q, w_o)
    tab = rel_bias.T
    zero_bias = jnp.zeros((D_MODEL,), F32)

    xp = x_prompt.reshape(bsz * seq, D_MODEL)
    xs = x_sample.reshape(n_tok_s, D_MODEL)
    tm_p, tm_f, tm_s = 512, 256, n_tok_s
    cache_kt, cache_vt = cache_k.swapaxes(2, 3), cache_v.swapaxes(2, 3)

    n_ffn = 2 * norm_ffn.shape[0]
    ffn_steps = bsz * seq // tm_f
    per_step = pages_per_ffn_step(n_ffn, ffn_steps, dbs, n_pages)
    flat_pages = page_table.reshape(-1)
    kmt_parts = []

    def ffn_prompt(x, layer, half):
        if not per_step:
            return ffn(x, norm_ffn[layer, half], w_ffn_in, w_ffn_out, layer, half, tm=tm_f)
        first = len(kmt_parts) * ffn_steps * per_step
        y, kmt = ffn(x, norm_ffn[layer, half], w_ffn_in, w_ffn_out, layer, half, tm=tm_f,
                     pages=(cache_kt, flat_pages, first, n_pages, per_step))
        kmt_parts.append(kmt)
        return y

    def ffn_sample(x, layer, half):
        return ffn(x, norm_ffn[layer, half], w_ffn_in, w_ffn_out, layer, half, tm=tm_s)

    xp = ffn_prompt(xp, 0, 0)
    xs = ffn_sample(xs, 0, 0)

    vp = glu(xp, norm_mix[0], w_pw1[0], b_pw1[0], tm=tm_p).reshape(bsz, seq, D_MODEL)
    vs = glu(xs, norm_mix[0], w_pw1[0], b_pw1[0], tm=tm_s).reshape(dbs, t_new, D_MODEL)
    prev_p = jnp.zeros((bsz, CONV_HALO, D_MODEL), F32)
    prev_s = jnp.pad(state_conv[0], ((0, 0), (CONV_PAD, 0), (0, 0)))
    vs_pad = jnp.pad(vs, ((0, 0), (0, SUBLANES - t_new), (0, 0)))
    yp, st_p = dwconv(prev_p, vp, w_dw[0], b_dw[0], ln_g[0], ln_b[0], ts=PAGE_SIZE, valid=PAGE_SIZE)
    ys, st_s = dwconv(prev_s, vs_pad, w_dw[0], b_dw[0], ln_g[0], ln_b[0], ts=SUBLANES, valid=t_new)
    conv_prompt = st_p[None, :, CONV_PAD:]
    conv_sample = st_s[None, :, CONV_PAD:]
    xp = linear_res(yp.reshape(bsz * seq, D_MODEL), w_pw2[0], b_pw2[0], xp, tm=tm_p)
    xs = linear_res(ys[:, :t_new].reshape(n_tok_s, D_MODEL), w_pw2[0], b_pw2[0], xs, tm=tm_s)

    xp = ffn_prompt(xp, 0, 1)
    xs = ffn_sample(xs, 0, 1)

    kt_pp, vt_pp, kn_p, kmt_p = shared_kv(xp.reshape(bsz, seq, D_MODEL), norm_kv, w_kv, k_gain)
    xs_blk = jnp.pad(xs, ((0, MOBA_BLOCK - n_tok_s), (0, 0)))[None]
    kt_sp, vt_sp, _, _ = shared_kv(xs_blk, norm_kv, w_kv, k_gain)

    def sample_cols(pages_t):
        return pages_t[0, 0, :, :, :n_tok_s].reshape(N_HEADS, HEAD_DIM, dbs, t_new).transpose(2, 0, 1, 3)

    knt_s, vnt_s = sample_cols(kt_sp), sample_cols(vt_sp)
    k_pp, v_pp = kt_pp.swapaxes(3, 4), vt_pp.swapaxes(3, 4)
    k_rows, v_rows = knt_s.swapaxes(2, 3), vnt_s.swapaxes(2, 3)

    xp = ffn_prompt(xp, 1, 0)
    xs = ffn_sample(xs, 1, 0)

    qt_p = q_proj(xp.reshape(bsz, seq, D_MODEL), norm_mix[1], w_q[0], q_gain[0], tm=tm_p)
    kmean_p = kmt_p.reshape(bsz, N_HEADS, HEAD_DIM, -1).swapaxes(2, 3)
    at_p = moba_prompt(tab, qt_p, kn_p, vt_pp, kmean_p)
    xp = linear_res(at_p, w_o[0], zero_bias, xp, tm=tm_p, transposed=True)
    xp = ffn_prompt(xp, 1, 1)

    qt_s = q_proj(xs[None], norm_mix[1], w_q[0], q_gain[0], tm=n_tok_s)[0]
    qt_s = qt_s.reshape(D_MODEL, dbs, t_new).transpose(1, 0, 2)
    kmt_s = jnp.concatenate(kmt_parts, axis=0) if per_step else page_block_means(cache_kt, page_table)
    sel = gate_topk(qt_s, kmt_s, page_table)
    at_s = moba_sample(qt_s.reshape(dbs, N_HEADS, HEAD_DIM, t_new), knt_s, vnt_s, cache_kt, cache_vt,
                       sel.reshape(-1), tab, past=past)
    a_s = at_s.transpose(0, 3, 1, 2).reshape(n_tok_s, D_MODEL)
    xs = linear_res(a_s, w_o[0], zero_bias, xs, tm=tm_s)

    xs = ffn_sample(xs, 1, 1)

    return (xp.reshape(bsz, seq, D_MODEL), xs.reshape(dbs, t_new, D_MODEL), conv_prompt, conv_sample,
            k_pp, v_pp, k_rows, v_rows)
```

```python
import functools
import math

import numpy as np
import jax
import jax.numpy as jnp
from jax import lax
from jax.experimental import pallas as pl
from jax.experimental.pallas import tpu as pltpu

D_MODEL = 1024
N_HEADS = 16
HEAD_DIM = D_MODEL // N_HEADS
D_FF = 2816
CONV_W = 31
MOBA_BLOCK = 256
TOP_K = 3
PAGE_SIZE = 128
PAGES_PER_BLOCK = MOBA_BLOCK // PAGE_SIZE
N_BUCKETS = 32
MAX_EXACT = N_BUCKETS // 2
REL_MAX_DIST = 2048
EPS = 1e-6
SCALE = HEAD_DIM ** -0.5

LANES = 128
SUBLANES = 8
VMEM_LIMIT = 56 * 1024 * 1024

F32 = jnp.float32
BF16 = jnp.bfloat16
NEG_INF = float("-inf")

CONV_HALO = 32
CONV_PAD = CONV_HALO - (CONV_W - 1)


def _first_far_distance():
    n = np.arange(1, 4 * REL_MAX_DIST, dtype=np.float32)
    large = MAX_EXACT + (np.log(n / np.float32(MAX_EXACT)) / np.float32(math.log(REL_MAX_DIST / MAX_EXACT))
                         * np.float32(N_BUCKETS - MAX_EXACT)).astype(np.int32)
    bucket = np.where(n < MAX_EXACT, n.astype(np.int32), np.minimum(large, N_BUCKETS - 1))
    return int(n[np.argmax(bucket == N_BUCKETS - 1)])


FAR_DIST = _first_far_distance()
FAR_BLOCKS = -(-(FAR_DIST + MOBA_BLOCK - 1) // MOBA_BLOCK)


def _cparams(*sem):
    return pltpu.CompilerParams(dimension_semantics=sem, vmem_limit_bytes=VMEM_LIMIT)


def _rms(x, g):
    ms = jnp.mean(x * x, axis=-1, keepdims=True)
    return x * lax.rsqrt(ms + EPS) * g


def _t5_bucket(delta):
    n = jnp.maximum(delta, 0)
    nf = jnp.maximum(n, 1).astype(F32)
    large = MAX_EXACT + (jnp.log(nf / MAX_EXACT) / math.log(REL_MAX_DIST / MAX_EXACT)
                         * (N_BUCKETS - MAX_EXACT)).astype(jnp.int32)
    large = jnp.minimum(large, N_BUCKETS - 1)
    return jnp.where(n < MAX_EXACT, n, large)


def _bias_from_bucket(bucket, tab_ref, h):
    val = jnp.zeros(bucket.shape, F32)
    for bb in range(N_BUCKETS):
        val = jnp.where(bucket == bb, tab_ref[h, bb], val)
    return val


MAX_PAGES_PER_STEP = 16


def _page_block_means(pages, o_ref, step, steps_per_seq):
    blocks_per_step = len(pages) // PAGES_PER_BLOCK
    sub = step % steps_per_seq

    @pl.when(sub == 0)
    def _():
        o_ref[...] = jnp.zeros_like(o_ref)

    col = lax.broadcasted_iota(jnp.int32, o_ref.shape, 1)
    out = o_ref[...]
    for j in range(blocks_per_step):
        tot = pages[j * PAGES_PER_BLOCK][...]
        for p in range(1, PAGES_PER_BLOCK):
            tot = tot + pages[j * PAGES_PER_BLOCK + p][...]
        mean = jnp.sum(tot, axis=-1, keepdims=True).reshape(D_MODEL, 1) * (1.0 / MOBA_BLOCK)
        out = jnp.where(col == sub * blocks_per_step + j, mean, out)
    o_ref[...] = out


def _ffn_kernel(*refs, tf, n_pages, steps_per_seq):
    if n_pages:
        refs = refs[1:]
    x_ref, g_ref, win_ref, wout_ref = refs[:4]
    pages = refs[4:4 + n_pages]
    o_ref = refs[4 + n_pages]
    h_ref = refs[-1]
    if n_pages:
        _page_block_means(pages, refs[5 + n_pages], pl.program_id(0), steps_per_seq)
    x = x_ref[...]
    h_ref[...] = _rms(x, g_ref[...]).astype(BF16)
    acc = None
    for j in range(D_FF // tf):
        h = h_ref[...]
        a = jnp.dot(h, win_ref[:, j * tf:(j + 1) * tf], preferred_element_type=F32)
        g = jnp.dot(h, win_ref[:, D_FF + j * tf:D_FF + (j + 1) * tf], preferred_element_type=F32)
        act = (a * jax.nn.sigmoid(a) * g).astype(BF16)
        part = jnp.dot(act, wout_ref[j * tf:(j + 1) * tf, :], preferred_element_type=F32)
        acc = part if acc is None else acc + part
    o_ref[...] = x + 0.5 * acc


def ffn(x, g, w_in, w_out, layer, half, *, tm, tf=D_FF // 2, pages=None):
    t = x.shape[0]
    steps = t // tm
    wmap = lambda i, *_: (layer, half, 0, 0)
    in_specs = [
        pl.BlockSpec((tm, D_MODEL), lambda i, *_: (i, 0)),
        pl.BlockSpec((1, D_MODEL), lambda i, *_: (0, 0)),
        pl.BlockSpec((None, None, D_MODEL, 2 * D_FF), wmap, pipeline_mode=pl.Buffered(1)),
        pl.BlockSpec((None, None, D_FF, D_MODEL), wmap, pipeline_mode=pl.Buffered(1)),
    ]
    x_spec = pl.BlockSpec((tm, D_MODEL), lambda i, *_: (i, 0))
    x_shape = jax.ShapeDtypeStruct((t, D_MODEL), F32)
    scratch = [pltpu.VMEM((tm, D_MODEL), BF16)]
    args = (x, g.reshape(1, D_MODEL), w_in, w_out)
    if pages is None:
        return pl.pallas_call(
            functools.partial(_ffn_kernel, tf=tf, n_pages=0, steps_per_seq=0),
            grid=(steps,), in_specs=in_specs, out_specs=x_spec, out_shape=x_shape, scratch_shapes=scratch,
            compiler_params=_cparams("parallel"), name="ffn",
        )(*args)
    cache_kt, page_table, first, n_pages_seq, per_step = pages
    n_full = n_pages_seq // PAGES_PER_BLOCK
    steps_per_seq = n_pages_seq // per_step

    def page_spec(p):
        return pl.BlockSpec((None, N_HEADS, HEAD_DIM, PAGE_SIZE),
                            lambda i, pt: (pt[first + i * per_step + p], 0, 0, 0))

    return pl.pallas_call(
        functools.partial(_ffn_kernel, tf=tf, n_pages=per_step, steps_per_seq=steps_per_seq),
        grid_spec=pltpu.PrefetchScalarGridSpec(
            num_scalar_prefetch=1,
            grid=(steps,),
            in_specs=in_specs + [page_spec(p) for p in range(per_step)],
            out_specs=[x_spec, pl.BlockSpec((None, D_MODEL, n_full), lambda i, pt: (i // steps_per_seq, 0, 0))],
            scratch_shapes=scratch,
        ),
        out_shape=[x_shape, jax.ShapeDtypeStruct((steps // steps_per_seq, D_MODEL, n_full), F32)],
        compiler_params=_cparams("arbitrary"),
        name="ffn_pages",
    )(page_table, *args, *([cache_kt] * per_step))


def pages_per_ffn_step(n_ffn, steps, n_seq, n_pages_seq):
    per_step, rem = divmod(n_seq * n_pages_seq, n_ffn * steps)
    ok = (rem == 0 and per_step > 0 and per_step % PAGES_PER_BLOCK == 0 and n_pages_seq % per_step == 0
          and steps % (n_pages_seq // per_step) == 0 and per_step <= MAX_PAGES_PER_STEP)
    return per_step if ok else 0


def _glu_kernel(x_ref, g_ref, w_ref, b_ref, o_ref):
    h = _rms(x_ref[...], g_ref[...]).astype(BF16)
    ag = jnp.dot(h, w_ref[...], preferred_element_type=F32) + b_ref[...]
    a = ag[:, :D_MODEL]
    gate = ag[:, D_MODEL:]
    o_ref[...] = a * jax.nn.sigmoid(gate)


def glu(x, g, w, b, *, tm):
    t = x.shape[0]
    return pl.pallas_call(
        _glu_kernel,
        grid=(t // tm,),
        in_specs=[
            pl.BlockSpec((tm, D_MODEL), lambda i: (i, 0)),
            pl.BlockSpec((1, D_MODEL), lambda i: (0, 0)),
            pl.BlockSpec((D_MODEL, 2 * D_MODEL), lambda i: (0, 0)),
            pl.BlockSpec((1, 2 * D_MODEL), lambda i: (0, 0)),
        ],
        out_specs=pl.BlockSpec((tm, D_MODEL), lambda i: (i, 0)),
        out_shape=jax.ShapeDtypeStruct((t, D_MODEL), F32),
        compiler_params=_cparams("parallel"),
        name="glu",
    )(x, g.reshape(1, D_MODEL), w, b.reshape(1, 2 * D_MODEL))


def _dwconv_kernel(prev_ref, v_ref, wdw_ref, bdw_ref, lng_ref, lnb_ref, y_ref, state_ref, buf_ref,
                   shift_ref, *, ts, valid):
    @pl.when(pl.program_id(1) == 0)
    def _():
        buf_ref[0:CONV_HALO, :] = prev_ref[...]

    buf_ref[CONV_HALO:CONV_HALO + ts, :] = v_ref[...]
    for c in range(D_MODEL // LANES):
        lanes = slice(c * LANES, (c + 1) * LANES)
        acc = jnp.zeros((ts, LANES), F32)
        for r in range(SUBLANES):
            taps = range(r, CONV_W, SUBLANES)
            span = taps[-1] - r + ts
            shift_ref[r % 2, 0:span, :] = buf_ref[CONV_PAD + r:CONV_PAD + r + span, lanes]
            for w in taps:
                acc = acc + shift_ref[r % 2, w - r:w - r + ts, :] * wdw_ref[w:w + 1, lanes]
        y_ref[:, lanes] = acc + bdw_ref[:, lanes]
    y = y_ref[...]
    mu = jnp.mean(y, axis=-1, keepdims=True)
    var = jnp.mean(jnp.square(y - mu), axis=-1, keepdims=True)
    yn = (y - mu) * lax.rsqrt(var + EPS) * lng_ref[...] + lnb_ref[...]
    y_ref[...] = yn * jax.nn.sigmoid(yn)
    state_ref[...] = buf_ref[valid:valid + CONV_HALO, :]
    buf_ref[0:CONV_HALO, :] = buf_ref[ts:ts + CONV_HALO, :]


def dwconv(prev, v, w_dw, b_dw, ln_g, ln_b, *, ts, valid):
    b, s, _ = v.shape
    row = lambda a: a.reshape(1, D_MODEL)
    return pl.pallas_call(
        functools.partial(_dwconv_kernel, ts=ts, valid=valid),
        grid=(b, s // ts),
        in_specs=[
            pl.BlockSpec((None, CONV_HALO, D_MODEL), lambda i, j: (i, 0, 0)),
            pl.BlockSpec((None, ts, D_MODEL), lambda i, j: (i, j, 0)),
            pl.BlockSpec((CONV_W, D_MODEL), lambda i, j: (0, 0)),
            pl.BlockSpec((1, D_MODEL), lambda i, j: (0, 0)),
            pl.BlockSpec((1, D_MODEL), lambda i, j: (0, 0)),
            pl.BlockSpec((1, D_MODEL), lambda i, j: (0, 0)),
        ],
        out_specs=[
            pl.BlockSpec((None, ts, D_MODEL), lambda i, j: (i, j, 0)),
            pl.BlockSpec((None, CONV_HALO, D_MODEL), lambda i, j: (i, 0, 0)),
        ],
        out_shape=[
            jax.ShapeDtypeStruct((b, s, D_MODEL), F32),
            jax.ShapeDtypeStruct((b, CONV_HALO, D_MODEL), F32),
        ],
        scratch_shapes=[pltpu.VMEM((CONV_HALO + ts, D_MODEL), F32),
                        pltpu.VMEM((2, CONV_HALO + ts, LANES), F32)],
        compiler_params=_cparams("parallel", "arbitrary"),
        name="dwconv",
    )(prev, v, w_dw, row(b_dw), row(ln_g), row(ln_b))


def _linear_res_kernel(y_ref, w_ref, b_ref, x_ref, o_ref, *, transposed):
    y = y_ref[...]
    if transposed:
        y = y.T
    o_ref[...] = x_ref[...] + jnp.dot(y.astype(BF16), w_ref[...], preferred_element_type=F32) + b_ref[...]


def linear_res(y, w, b, x, *, tm, transposed=False):
    t = x.shape[0]
    if transposed:
        tiles = y.shape[2] // tm
        y_spec = pl.BlockSpec((None, D_MODEL, tm), lambda i: (i // tiles, 0, i % tiles))
    else:
        y_spec = pl.BlockSpec((tm, D_MODEL), lambda i: (i, 0))
    return pl.pallas_call(
        functools.partial(_linear_res_kernel, transposed=transposed),
        grid=(t // tm,),
        in_specs=[
            y_spec,
            pl.BlockSpec((D_MODEL, D_MODEL), lambda i: (0, 0)),
            pl.BlockSpec((1, D_MODEL), lambda i: (0, 0)),
            pl.BlockSpec((tm, D_MODEL), lambda i: (i, 0)),
        ],
        out_specs=pl.BlockSpec((tm, D_MODEL), lambda i: (i, 0)),
        out_shape=jax.ShapeDtypeStruct((t, D_MODEL), F32),
        compiler_params=_cparams("parallel"),
        name="linear_res",
    )(y, w, b.reshape(1, D_MODEL), x)


def _kv_kernel(x_ref, g_ref, w_ref, kgain_ref, kt_ref, vt_ref, kn_ref, kmt_ref, knt_ref):
    j = pl.program_id(1)
    h = _rms(x_ref[...], g_ref[...]).astype(BF16)
    kv = jnp.dot(h, w_ref[...], preferred_element_type=F32)
    kt = kv[:, :D_MODEL].T
    vt = kv[:, D_MODEL:].T
    for hd in range(N_HEADS):
        rows = slice(hd * HEAD_DIM, (hd + 1) * HEAD_DIM)
        kh = kt[rows, :]
        ms = jnp.mean(kh * kh, axis=0, keepdims=True)
        khn = kh * lax.rsqrt(ms + EPS) * kgain_ref[...]
        knt_ref[rows, :] = khn
        for p in range(PAGES_PER_BLOCK):
            toks = slice(p * PAGE_SIZE, (p + 1) * PAGE_SIZE)
            kt_ref[p, hd] = khn[:, toks]
            vt_ref[p, hd] = vt[rows, toks]
    knt = knt_ref[...]
    kn_ref[...] = knt.T.astype(BF16)
    kmean = jnp.sum(knt, axis=1, keepdims=True) * (1.0 / MOBA_BLOCK)

    @pl.when(j == 0)
    def _():
        kmt_ref[...] = jnp.zeros_like(kmt_ref)

    col = lax.broadcasted_iota(jnp.int32, kmt_ref.shape, 1)
    kmt_ref[...] = jnp.where(col == j, kmean, kmt_ref[...])


def shared_kv(x, g, w, k_gain):
    b, s, _ = x.shape
    n_blk = s // MOBA_BLOCK
    pages = (b, s // PAGE_SIZE, N_HEADS, HEAD_DIM, PAGE_SIZE)
    page_spec = pl.BlockSpec((None, PAGES_PER_BLOCK, N_HEADS, HEAD_DIM, PAGE_SIZE),
                             lambda i, j: (i, j, 0, 0, 0))
    return pl.pallas_call(
        _kv_kernel,
        grid=(b, n_blk),
        in_specs=[
            pl.BlockSpec((None, MOBA_BLOCK, D_MODEL), lambda i, j: (i, j, 0)),
            pl.BlockSpec((1, D_MODEL), lambda i, j: (0, 0)),
            pl.BlockSpec((D_MODEL, 2 * D_MODEL), lambda i, j: (0, 0)),
            pl.BlockSpec((HEAD_DIM, 1), lambda i, j: (0, 0)),
        ],
        out_specs=[
            page_spec,
            page_spec,
            pl.BlockSpec((None, MOBA_BLOCK, D_MODEL), lambda i, j: (i, j, 0)),
            pl.BlockSpec((None, D_MODEL, n_blk), lambda i, j: (i, 0, 0)),
        ],
        out_shape=[
            jax.ShapeDtypeStruct(pages, F32),
            jax.ShapeDtypeStruct(pages, F32),
            jax.ShapeDtypeStruct((b, s, D_MODEL), BF16),
            jax.ShapeDtypeStruct((b, D_MODEL, n_blk), F32),
        ],
        scratch_shapes=[pltpu.VMEM((D_MODEL, MOBA_BLOCK), F32)],
        compiler_params=_cparams("parallel", "arbitrary"),
        name="shared_kv",
    )(x, g.reshape(1, D_MODEL), w, k_gain.reshape(HEAD_DIM, 1))


def _q_kernel(x_ref, g_ref, w_ref, qgain_ref, qt_ref):
    h = _rms(x_ref[...], g_ref[...]).astype(BF16)
    qt = jnp.dot(h, w_ref[...], preferred_element_type=F32).T
    for hd in range(N_HEADS):
        rows = slice(hd * HEAD_DIM, (hd + 1) * HEAD_DIM)
        qh = qt[rows, :]
        ms = jnp.mean(qh * qh, axis=0, keepdims=True)
        qt_ref[rows, :] = qh * lax.rsqrt(ms + EPS) * (qgain_ref[...] * SCALE)


def q_proj(x, g, w, q_gain, *, tm):
    b, s, _ = x.shape
    return pl.pallas_call(
        _q_kernel,
        grid=(b, s // tm),
        in_specs=[
            pl.BlockSpec((None, tm, D_MODEL), lambda i, j: (i, j, 0)),
            pl.BlockSpec((1, D_MODEL), lambda i, j: (0, 0)),
            pl.BlockSpec((D_MODEL, D_MODEL), lambda i, j: (0, 0)),
            pl.BlockSpec((HEAD_DIM, 1), lambda i, j: (0, 0)),
        ],
        out_specs=pl.BlockSpec((None, D_MODEL, tm), lambda i, j: (i, 0, j)),
        out_shape=jax.ShapeDtypeStruct((b, D_MODEL, s), F32),
        compiler_params=_cparams("parallel", "parallel"),
        name="q_proj",
    )(x, g.reshape(1, D_MODEL), w, q_gain.reshape(HEAD_DIM, 1))


FAR_GROUP = 2
NEAR_SPLIT = 4
HEAD_PAIR = 2
LOG2E = math.log2(math.e)
ONES_ROWS = 16


def _moba_prompt_kernel(tab_ref, qt_ref, kn_ref, vt_ref, kmean_ref, o_ref, bias_ref, sel_ref, near_ref,
                        far0_ref, far1_ref, *, n_blk):
    hp = pl.program_id(0)
    blk = MOBA_BLOCK
    wide = HEAD_PAIR * blk
    kk = lax.broadcasted_iota(jnp.int32, (blk, blk), 0)
    qq = lax.broadcasted_iota(jnp.int32, (blk, blk), 1)

    @pl.when(pl.program_id(1) == 0)
    def _():
        for d in range(FAR_BLOCKS):
            bucket = _t5_bucket(d * blk + qq - kk)
            for e in range(HEAD_PAIR):
                bias_ref[d, :, e * blk:(e + 1) * blk] = (
                    _bias_from_bucket(bucket, tab_ref, HEAD_PAIR * hp + e) * LOG2E)

    far_bias = jnp.concatenate(
        [jnp.full((1, blk), tab_ref[HEAD_PAIR * hp + e, N_BUCKETS - 1] * LOG2E, F32) for e in range(HEAD_PAIR)],
        axis=1)
    causal = jnp.concatenate([kk <= qq] * HEAD_PAIR, axis=1)
    blk_f = lax.broadcasted_iota(jnp.int32, (n_blk, wide), 0).astype(F32)
    zeros_q = jnp.zeros((HEAD_DIM, blk), F32)

    def q_block(c, carry):
        q0 = pl.multiple_of(c * blk, blk)
        qt = qt_ref[:, pl.ds(q0, blk)]
        qs = [qt[e * HEAD_DIM:(e + 1) * HEAD_DIM] for e in range(HEAD_PAIR)]
        q2 = jnp.concatenate(
            [jnp.concatenate([qs[0] * LOG2E, zeros_q], axis=0), jnp.concatenate([zeros_q, qs[1] * LOG2E], axis=0)],
            axis=1).astype(BF16)
        gate = jnp.concatenate(
            [jnp.dot(kmean_ref[e], qs[e], preferred_element_type=F32, precision=lax.Precision.HIGHEST)
             for e in range(HEAD_PAIR)], axis=1)
        past = blk_f < lax.convert_element_type(c, F32)
        g = jnp.where(past, gate, NEG_INF)
        addmask = jnp.full((n_blk, wide), NEG_INF, F32)
        for _ in range(TOP_K):
            top = jnp.max(g, axis=0, keepdims=True)
            first = jnp.min(jnp.where(g == top, blk_f, float(n_blk)), axis=0, keepdims=True)
            pick = blk_f == first
            addmask = jnp.where(pick, jnp.where(past, 0.0, NEG_INF), addmask)
            g = jnp.where(pick, NEG_INF, g)
        sel_ref[...] = addmask

        def scores(kb):
            keys = kn_ref[pl.ds(pl.multiple_of(kb * blk, blk), blk), :]
            return jnp.dot(keys, q2, preferred_element_type=F32)

        def values(kb, e):
            pages = [vt_ref[kb * PAGES_PER_BLOCK + p, e] for p in range(PAGES_PER_BLOCK)]
            return jnp.concatenate(pages, axis=1).astype(BF16)

        def attend(ma, s_ref, rows, kbs, first=0):
            m, acc = ma
            tops = [jnp.max(s_ref[first + i], axis=0, keepdims=True) for i in range(len(kbs))]
            tops = [t if row is None else t + row for t, row in zip(tops, rows)]
            m_new = functools.reduce(jnp.maximum, tops, m)
            ps = jnp.concatenate(
                [jnp.exp2(s_ref[first + i] - (m_new if row is None else m_new - row)).astype(BF16)
                 for i, row in enumerate(rows)], axis=0)
            ones = jnp.ones((ONES_ROWS, ps.shape[0]), BF16)
            pvs = []
            for e in range(HEAD_PAIR):
                vals = jnp.concatenate([values(kb, e) for kb in kbs], axis=1)
                pvs.append(jnp.dot(jnp.concatenate([vals, ones], axis=0), ps[:, e * blk:(e + 1) * blk],
                                   preferred_element_type=F32))
            return m_new, jnp.exp2(m - m_new) * acc + jnp.concatenate(pvs, axis=1)

        n_far = jnp.maximum(c - (FAR_BLOCKS - 1), 0)

        def far_scores(g, s_ref):
            for j in range(FAR_GROUP):
                s_ref[j] = scores(jnp.minimum(g * FAR_GROUP + j, n_blk - 1))

        def far_attend(g, s_ref, ma):
            kbs = [g * FAR_GROUP + j for j in range(FAR_GROUP)]
            kbc = [jnp.minimum(kb, n_blk - 1) for kb in kbs]
            rows = [jnp.where(kb < n_far, sel_ref[pl.ds(k, 1), :] + far_bias, NEG_INF)
                    for kb, k in zip(kbs, kbc)]
            return attend(ma, s_ref, rows, kbc)

        far_scores(0, far0_ref)
        near_ref[0] = jnp.where(causal, scores(c) + bias_ref[0], NEG_INF)
        rows, kbs = [None], [c]
        for d in range(1, FAR_BLOCKS):
            kb = jnp.maximum(c - d, 0)
            near_ref[d] = scores(kb) + bias_ref[d]
            rows.append(jnp.where(c - d >= 0, sel_ref[pl.ds(kb, 1), :], NEG_INF))
            kbs.append(kb)
        init = (jnp.full((1, wide), NEG_INF, F32), jnp.zeros((HEAD_DIM + ONES_ROWS, wide), F32))
        ma = attend(init, near_ref, rows[:NEAR_SPLIT], kbs[:NEAR_SPLIT])
        ma = attend(ma, near_ref, rows[NEAR_SPLIT:], kbs[NEAR_SPLIT:], first=NEAR_SPLIT)

        def far_pair(i, ma):
            far_scores(2 * i + 1, far1_ref)
            ma = far_attend(2 * i, far0_ref, ma)
            far_scores(2 * i + 2, far0_ref)
            return far_attend(2 * i + 1, far1_ref, ma)

        n_groups = (n_far + FAR_GROUP - 1) // FAR_GROUP
        _, acc = lax.fori_loop(0, (n_groups + 1) // 2, far_pair, ma)
        for e in range(HEAD_PAIR):
            cols = slice(e * blk, (e + 1) * blk)
            o_ref[e * HEAD_DIM:(e + 1) * HEAD_DIM, pl.ds(q0, blk)] = (
                acc[:HEAD_DIM, cols] / acc[HEAD_DIM:HEAD_DIM + 1, cols])
        return carry

    lax.fori_loop(0, n_blk, q_block, 0)


def moba_prompt(tab, qt, kn, vt_pages, kmean):
    b, _, s = qt.shape
    n_blk = s // MOBA_BLOCK
    wide = HEAD_PAIR * MOBA_BLOCK
    pair_t = pl.BlockSpec((None, HEAD_PAIR * HEAD_DIM, s), lambda hp, i: (i, hp, 0))
    return pl.pallas_call(
        functools.partial(_moba_prompt_kernel, n_blk=n_blk),
        grid=(N_HEADS // HEAD_PAIR, b),
        in_specs=[
            pl.BlockSpec(memory_space=pltpu.SMEM),
            pair_t,
            pl.BlockSpec((None, s, HEAD_PAIR * HEAD_DIM), lambda hp, i: (i, 0, hp)),
            pl.BlockSpec((None, s // PAGE_SIZE, HEAD_PAIR, HEAD_DIM, PAGE_SIZE), lambda hp, i: (i, 0, hp, 0, 0)),
            pl.BlockSpec((None, HEAD_PAIR, n_blk, HEAD_DIM), lambda hp, i: (i, hp, 0, 0)),
        ],
        out_specs=pair_t,
        out_shape=jax.ShapeDtypeStruct((b, D_MODEL, s), F32),
        scratch_shapes=[
            pltpu.VMEM((FAR_BLOCKS, MOBA_BLOCK, wide), F32),
            pltpu.VMEM((n_blk, wide), F32),
            pltpu.VMEM((FAR_BLOCKS, MOBA_BLOCK, wide), F32),
            pltpu.VMEM((FAR_GROUP, MOBA_BLOCK, wide), F32),
            pltpu.VMEM((FAR_GROUP, MOBA_BLOCK, wide), F32),
        ],
        compiler_params=_cparams("arbitrary", "arbitrary"),
        name="moba_prompt",
    )(tab, qt, kn, vt_pages, kmean)


PAGES_PER_STEP = 16


def _page_mean_kernel(pt_ref, *refs):
    _page_block_means(refs[:-1], refs[-1], pl.program_id(1), pl.num_programs(1))


def page_block_means(cache_kt, page_table):
    db, n_pages = page_table.shape
    assert n_pages % PAGES_PER_STEP == 0
    n_full = n_pages // PAGES_PER_BLOCK

    def page_spec(p):
        return pl.BlockSpec((None, N_HEADS, HEAD_DIM, PAGE_SIZE),
                            lambda i, j, pt: (pt[i * n_pages + j * PAGES_PER_STEP + p], 0, 0, 0))

    return pl.pallas_call(
        _page_mean_kernel,
        grid_spec=pltpu.PrefetchScalarGridSpec(
            num_scalar_prefetch=1,
            grid=(db, n_pages // PAGES_PER_STEP),
            in_specs=[page_spec(p) for p in range(PAGES_PER_STEP)],
            out_specs=pl.BlockSpec((None, D_MODEL, n_full), lambda i, j, pt: (i, 0, 0)),
        ),
        out_shape=jax.ShapeDtypeStruct((db, D_MODEL, n_full), F32),
        compiler_params=_cparams("parallel", "arbitrary"),
        name="page_block_means",
    )(page_table.reshape(-1), *([cache_kt] * PAGES_PER_STEP))


def _gate_topk_kernel(qt_ref, kmt_ref, segt_ref, pt_ref, o_ref, *, n_full, t_new):
    kmt = kmt_ref[...]
    blk_f = lax.broadcasted_iota(jnp.int32, (N_HEADS, n_full), 1).astype(F32)
    for t in range(t_new):
        prod = kmt * qt_ref[:, t:t + 1]
        g = jnp.dot(segt_ref[...], prod, preferred_element_type=F32, precision=lax.Precision.HIGHEST)
        for j in range(TOP_K):
            top = jnp.max(g, axis=1, keepdims=True)
            first = jnp.min(jnp.where(g == top, blk_f, float(n_full)), axis=1, keepdims=True)
            pick = blk_f == first
            base = (t * TOP_K + j) * (1 + PAGES_PER_BLOCK)
            o_ref[:, base:base + 1] = first.astype(jnp.int32)
            for p in range(PAGES_PER_BLOCK):
                page = jnp.sum(jnp.where(pick, pt_ref[p:p + 1, :], 0.0), axis=1, keepdims=True)
                o_ref[:, base + 1 + p:base + 2 + p] = page.astype(jnp.int32)
            g = jnp.where(pick, NEG_INF, g)


def gate_topk(qt, kmt, page_table):
    db, _, t_new = qt.shape
    n_full = kmt.shape[2]
    width = t_new * TOP_K * (1 + PAGES_PER_BLOCK)
    segt = (jnp.arange(D_MODEL)[None, :] // HEAD_DIM == jnp.arange(N_HEADS)[:, None]).astype(F32)
    pt = page_table.reshape(db, n_full, PAGES_PER_BLOCK).transpose(0, 2, 1).astype(F32)
    out = pl.pallas_call(
        functools.partial(_gate_topk_kernel, n_full=n_full, t_new=t_new),
        grid=(db,),
        in_specs=[
            pl.BlockSpec((None, D_MODEL, t_new), lambda i: (i, 0, 0)),
            pl.BlockSpec((None, D_MODEL, n_full), lambda i: (i, 0, 0)),
            pl.BlockSpec((N_HEADS, D_MODEL), lambda i: (0, 0)),
            pl.BlockSpec((None, PAGES_PER_BLOCK, n_full), lambda i: (i, 0, 0)),
        ],
        out_specs=pl.BlockSpec((None, N_HEADS, width), lambda i: (i, 0, 0)),
        out_shape=jax.ShapeDtypeStruct((db, N_HEADS, width), jnp.int32),
        compiler_params=_cparams("parallel"),
        name="gate_topk",
    )(qt, kmt, segt, pt)
    return out.reshape(db, N_HEADS, t_new, TOP_K, 1 + PAGES_PER_BLOCK)


def _moba_sample_kernel(sel_ref, tab_ref, qt_ref, knt_ref, vnt_ref, ck_hbm, cv_hbm, o_ref, kbuf, vbuf, sems,
                        *, t_new, past, n_steps):
    n_sel_pages = t_new * TOP_K * PAGES_PER_BLOCK
    width = t_new * TOP_K * (1 + PAGES_PER_BLOCK)
    b = pl.program_id(0)
    h = pl.program_id(1)
    step = b * N_HEADS + h
    slot = step % 2

    def page_copies(st, sl):
        head = st % N_HEADS
        copies = []
        for i in range(n_sel_pages):
            tj, p = divmod(i, PAGES_PER_BLOCK)
            page = sel_ref[st * width + tj * (1 + PAGES_PER_BLOCK) + 1 + p]
            copies.append(pltpu.make_async_copy(ck_hbm.at[page, head], kbuf.at[sl, i], sems.at[0, sl]))
            copies.append(pltpu.make_async_copy(cv_hbm.at[page, head], vbuf.at[sl, i], sems.at[1, sl]))
        return copies

    def start_all(copies):
        for n, cp in enumerate(copies):
            cp.start(priority=n % 2)

    @pl.when(step == 0)
    def _():
        start_all(page_copies(0, 0))

    @pl.when(step + 1 < n_steps)
    def _():
        start_all(page_copies(step + 1, 1 - slot))

    for cp in page_copies(step, slot):
        cp.wait()
    k_pages = [kbuf.at[slot, i] for i in range(n_sel_pages)]
    v_pages = [vbuf.at[slot, i] for i in range(n_sel_pages)]
    new_id = lax.broadcasted_iota(jnp.int32, (1, t_new), 1)
    row_id = lax.broadcasted_iota(jnp.int32, (1, PAGE_SIZE), 1)
    knt = knt_ref[...]
    vnt = vnt_ref[...]
    stride = TOP_K * (1 + PAGES_PER_BLOCK)
    page_row = lax.broadcasted_iota(jnp.int32, (n_sel_pages, 1), 0)
    dist0 = jnp.zeros((n_sel_pages, 1), jnp.int32)
    for t in range(t_new):
        for j in range(TOP_K):
            idx = sel_ref[((b * N_HEADS + h) * t_new + t) * stride + j * (1 + PAGES_PER_BLOCK)]
            for p in range(PAGES_PER_BLOCK):
                pg = (t * TOP_K + j) * PAGES_PER_BLOCK + p
                dist0 = jnp.where(page_row == pg, past + t - (idx * MOBA_BLOCK + p * PAGE_SIZE), dist0)
    page_bias = _bias_from_bucket(_t5_bucket(dist0 - row_id), tab_ref, h)
    own_bias = _bias_from_bucket(
        _t5_bucket(lax.broadcasted_iota(jnp.int32, (t_new, 1), 0) - new_id), tab_ref, h)
    for t in range(t_new):
        q = qt_ref[:, t:t + 1]
        s_own = jnp.sum(knt * q, axis=0, keepdims=True) + own_bias[t:t + 1, :]
        s_own = jnp.where(new_id <= t, s_own, NEG_INF)
        scores = []
        for pg in range(t * TOP_K * PAGES_PER_BLOCK, (t + 1) * TOP_K * PAGES_PER_BLOCK):
            s = jnp.sum(k_pages[pg][...] * q, axis=0, keepdims=True)
            scores.append(s + page_bias[pg:pg + 1, :])
        m = jnp.max(s_own, axis=1, keepdims=True)
        for s in scores:
            m = jnp.maximum(m, jnp.max(s, axis=1, keepdims=True))
        e_own = jnp.exp(s_own - m)
        den = jnp.sum(e_own, axis=1, keepdims=True)
        num = jnp.sum(vnt * e_own, axis=1, keepdims=True)
        weighted = jnp.zeros((HEAD_DIM, PAGE_SIZE), F32)
        for i, s in enumerate(scores):
            e = jnp.exp(s - m)
            den = den + jnp.sum(e, axis=1, keepdims=True)
            weighted = weighted + v_pages[t * TOP_K * PAGES_PER_BLOCK + i][...] * e
        num = num + jnp.sum(weighted, axis=1, keepdims=True)
        o_ref[:, t:t + 1] = num / den


def moba_sample(qt, knt, vnt, cache_kt, cache_vt, sel, tab, *, past):
    db, _, _, t_new = qt.shape
    n_sel_pages = t_new * TOP_K * PAGES_PER_BLOCK
    col_spec = pl.BlockSpec((None, None, HEAD_DIM, t_new), lambda i, h, se: (i, h, 0, 0))
    page_buf = pltpu.VMEM((2, n_sel_pages, HEAD_DIM, PAGE_SIZE), F32)
    return pl.pallas_call(
        functools.partial(_moba_sample_kernel, t_new=t_new, past=past, n_steps=db * N_HEADS),
        grid_spec=pltpu.PrefetchScalarGridSpec(
            num_scalar_prefetch=1,
            grid=(db, N_HEADS),
            in_specs=[pl.BlockSpec(memory_space=pltpu.SMEM), col_spec, col_spec, col_spec,
                      pl.BlockSpec(memory_space=pl.ANY), pl.BlockSpec(memory_space=pl.ANY)],
            out_specs=col_spec,
            scratch_shapes=[page_buf, page_buf, pltpu.SemaphoreType.DMA((2, 2))],
        ),
        out_shape=jax.ShapeDtypeStruct((db, N_HEADS, HEAD_DIM, t_new), F32),
        compiler_params=_cparams("arbitrary", "arbitrary"),
        name="moba_sample",
    )(sel, tab, qt, knt, vnt, cache_kt, cache_vt)


def _bf16_weights(w_ffn_in, w_ffn_out, w_pw1, w_pw2, w_kv, w_q, w_o):
    return tuple(w.astype(BF16) for w in (w_ffn_in, w_ffn_out, w_pw1, w_pw2, w_kv, w_q, w_o))


def kernel(x_prompt, x_sample, state_conv, cache_k, cache_v, page_table, norm_ffn, w_ffn_in, w_ffn_out,
           norm_mix, w_pw1, b_pw1, w_dw, b_dw, ln_g, ln_b, w_pw2, b_pw2, norm_kv, w_kv, k_gain, w_q, q_gain,
           w_o, rel_bias):
    bsz, seq, _ = x_prompt.shape
    dbs, t_new, _ = x_sample.shape
    n_pages = page_table.shape[1]
    past = n_pages * PAGE_SIZE
    assert past % MOBA_BLOCK == 0 and t_new <= SUBLANES and seq % MOBA_BLOCK == 0
    n_tok_s = dbs * t_new
    w_ffn_in, w_ffn_out, w_pw1, w_pw2, w_kv, w_q, w_o = _bf16_weights(
        w_ffn_in, w_ffn_out, w_pw1, w_pw2, w_kv, w_q, w_o)
    tab = rel_bias.T
    zero_bias = jnp.zeros((D_MODEL,), F32)

    xp = x_prompt.reshape(bsz * seq, D_MODEL)
    xs = x_sample.reshape(n_tok_s, D_MODEL)
    tm_p, tm_f, tm_s = 512, 256, n_tok_s
    cache_kt, cache_vt = cache_k.swapaxes(2, 3), cache_v.swapaxes(2, 3)

    n_ffn = 2 * norm_ffn.shape[0]
    ffn_steps = bsz * seq // tm_f
    per_step = pages_per_ffn_step(n_ffn, ffn_steps, dbs, n_pages)
    flat_pages = page_table.reshape(-1)
    kmt_parts = []

    def ffn_prompt(x, layer, half):
        if not per_step:
            return ffn(x, norm_ffn[layer, half], w_ffn_in, w_ffn_out, layer, half, tm=tm_f)
        first = len(kmt_parts) * ffn_steps * per_step
        y, kmt = ffn(x, norm_ffn[layer, half], w_ffn_in, w_ffn_out, layer, half, tm=tm_f,
                     pages=(cache_kt, flat_pages, first, n_pages, per_step))
        kmt_parts.append(kmt)
        return y

    def ffn_sample(x, layer, half):
        return ffn(x, norm_ffn[layer, half], w_ffn_in, w_ffn_out, layer, half, tm=tm_s)

    xp = ffn_prompt(xp, 0, 0)
    xs = ffn_sample(xs, 0, 0)

    vp = glu(xp, norm_mix[0], w_pw1[0], b_pw1[0], tm=tm_p).reshape(bsz, seq, D_MODEL)
    vs = glu(xs, norm_mix[0], w_pw1[0], b_pw1[0], tm=tm_s).reshape(dbs, t_new, D_MODEL)
    prev_p = jnp.zeros((bsz, CONV_HALO, D_MODEL), F32)
    prev_s = jnp.pad(state_conv[0], ((0, 0), (CONV_PAD, 0), (0, 0)))
    vs_pad = jnp.pad(vs, ((0, 0), (0, SUBLANES - t_new), (0, 0)))
    yp, st_p = dwconv(prev_p, vp, w_dw[0], b_dw[0], ln_g[0], ln_b[0], ts=PAGE_SIZE, valid=PAGE_SIZE)
    ys, st_s = dwconv(prev_s, vs_pad, w_dw[0], b_dw[0], ln_g[0], ln_b[0], ts=SUBLANES, valid=t_new)
    conv_prompt = st_p[None, :, CONV_PAD:]
    conv_sample = st_s[None, :, CONV_PAD:]
    xp = linear_res(yp.reshape(bsz * seq, D_MODEL), w_pw2[0], b_pw2[0], xp, tm=tm_p)
    xs = linear_res(ys[:, :t_new].reshape(n_tok_s, D_MODEL), w_pw2[0], b_pw2[0], xs, tm=tm_s)

    xp = ffn_prompt(xp, 0, 1)
    xs = ffn_sample(xs, 0, 1)

    kt_pp, vt_pp, kn_p, kmt_p = shared_kv(xp.reshape(bsz, seq, D_MODEL), norm_kv, w_kv, k_gain)
    xs_blk = jnp.pad(xs, ((0, MOBA_BLOCK - n_tok_s), (0, 0)))[None]
    kt_sp, vt_sp, _, _ = shared_kv(xs_blk, norm_kv, w_kv, k_gain)

    def sample_cols(pages_t):
        return pages_t[0, 0, :, :, :n_tok_s].reshape(N_HEADS, HEAD_DIM, dbs, t_new).transpose(2, 0, 1, 3)

    knt_s, vnt_s = sample_cols(kt_sp), sample_cols(vt_sp)
    k_pp, v_pp = kt_pp.swapaxes(3, 4), vt_pp.swapaxes(3, 4)
    k_rows, v_rows = knt_s.swapaxes(2, 3), vnt_s.swapaxes(2, 3)

    xp = ffn_prompt(xp, 1, 0)
    xs = ffn_sample(xs, 1, 0)

    qt_p = q_proj(xp.reshape(bsz, seq, D_MODEL), norm_mix[1], w_q[0], q_gain[0], tm=tm_p)
    kmean_p = kmt_p.reshape(bsz, N_HEADS, HEAD_DIM, -1).swapaxes(2, 3)
    at_p = moba_prompt(tab, qt_p, kn_p, vt_pp, kmean_p)
    xp = linear_res(at_p, w_o[0], zero_bias, xp, tm=tm_p, transposed=True)
    xp = ffn_prompt(xp, 1, 1)

    qt_s = q_proj(xs[None], norm_mix[1], w_q[0], q_gain[0], tm=n_tok_s)[0]
    qt_s = qt_s.reshape(D_MODEL, dbs, t_new).transpose(1, 0, 2)
    kmt_s = jnp.concatenate(kmt_parts, axis=0) if per_step else page_block_means(cache_kt, page_table)
    sel = gate_topk(qt_s, kmt_s, page_table)
    at_s = moba_sample(qt_s.reshape(dbs, N_HEADS, HEAD_DIM, t_new), knt_s, vnt_s, cache_kt, cache_vt,
                       sel.reshape(-1), tab, past=past)
    a_s = at_s.transpose(0, 3, 1, 2).reshape(n_tok_s, D_MODEL)
    xs = linear_res(a_s, w_o[0], zero_bias, xs, tm=tm_s)

    xs = ffn_sample(xs, 1, 1)

    return (xp.reshape(bsz, seq, D_MODEL), xs.reshape(dbs, t_new, D_MODEL), conv_prompt, conv_sample,
            k_pp, v_pp, k_rows, v_rows)
```
